```python
import math
import jax, jax.numpy as jnp
from jax import lax
import numpy as np

D_MODEL = 4096
BATCH = 4
SEQ = 2048
DEPTH = 2
DEC_BATCH = 8
DEC_SEQ = 4
PAST_LEN = 16384
PAGE_SIZE = 128

HEAD_DIM = 128
N_HEADS = D_MODEL // HEAD_DIM
H_M = N_HEADS // 4
H_A = N_HEADS // 2
D_M = H_M * HEAD_DIM
D_A = H_A * HEAD_DIM
D_C = D_MODEL - D_M - D_A
N_CONV_GROUPS = D_C // HEAD_DIM
CONV_WIDTH = 3
MOBA_BLOCK = 256
MOBA_TOPK = 3
MOBA_Q_CHUNK = 32
MLSTM_CHUNK = 64
D_FF = ((8 * D_MODEL // 3 + 255) // 256) * 256
PLE_DIM = 256
RMS_EPS = 1e-6
FORGET_BIAS_LO = 3.0
FORGET_BIAS_HI = 6.0
SPLITS = (D_M, D_M, D_M, D_M, H_M, H_M, D_A, D_A, D_A, D_C, D_C, D_C)
N_IN = sum(SPLITS)

kernel_name = 'hybrid_mlstm_moba_shortconv_decode_step'


def rmsnorm(x, g):
    xf = x.astype(jnp.float32)
    y = xf * lax.rsqrt(jnp.mean(xf * xf, axis=-1, keepdims=True) + RMS_EPS)
    return (y * g.astype(jnp.float32)).astype(x.dtype)


def swiglu(x, w1, w3, w2):
    return (jax.nn.silu(x @ w1) * (x @ w3)) @ w2


def split_cols(z):
    outs, off = [], 0
    for s in SPLITS:
        outs.append(z[..., off:off + s])
        off += s
    return outs


def mlstm_chunkwise(q, k, v, i_pre, f_pre, c0, n0, m0):
    f32 = jnp.float32
    B, T, H, Dh = q.shape
    lc = math.gcd(T, MLSTM_CHUNK)
    nc = T // lc

    def to_chunks(a):
        a = a.astype(f32).reshape((B, nc, lc) + a.shape[2:])
        return jnp.moveaxis(jnp.moveaxis(a, 1, 0), 3, 2)

    qs, ks, vs = to_chunks(q), to_chunks(k) * (Dh ** -0.5), to_chunks(v)
    igs, fgs = to_chunks(i_pre), to_chunks(f_pre)
    causal = jnp.tril(jnp.ones((lc, lc), dtype=bool))

    def step(carry, xs):
        C, n, m = carry
        qc, kc, vc, ic, fc = xs
        F = jnp.cumsum(jax.nn.log_sigmoid(fc), axis=-1)
        dmat = F[..., :, None] - F[..., None, :] + ic[..., None, :]
        dmat = jnp.where(causal, dmat, -jnp.inf)
        inter = F + m[..., None]
        m_t = jnp.maximum(inter, jnp.max(dmat, axis=-1))
        w_intra = jnp.exp(dmat - m_t[..., None])
        w_inter = jnp.exp(inter - m_t)
        a = w_intra * jnp.einsum('bhtd,bhsd->bhts', qc, kc)
        num = w_inter[..., None] * jnp.einsum('bhvk,bhtk->bhtv', C, qc) + jnp.einsum('bhts,bhsv->bhtv', a, vc)
        den = w_inter * jnp.einsum('bhk,bhtk->bht', n, qc) + jnp.sum(a, axis=-1)
        h = num / jnp.maximum(jnp.abs(den), jnp.exp(-m_t))[..., None]
        m_end = m_t[..., -1]
        w_s = jnp.exp(F[..., -1:] - F + ic - m_end[..., None])
        decay = jnp.exp(inter[..., -1] - m_end)
        C_new = decay[..., None, None] * C + jnp.einsum('bhs,bhsv,bhsk->bhvk', w_s, vc, kc)
        n_new = decay[..., None] * n + jnp.einsum('bhs,bhsk->bhk', w_s, kc)
        return (C_new, n_new, m_end), h

    (c1, n1, m1), hs = lax.scan(step, (c0.astype(f32), n0.astype(f32), m0.astype(f32)),
                                (qs, ks, vs, igs, fgs))
    h = jnp.swapaxes(jnp.moveaxis(hs, 0, 1), 2, 3).reshape(B, T, H, Dh)
    return h, c1, n1, m1


def moba_attention(q, k, v, q_start):
    f32 = jnp.float32
    B, T, H, Dh = q.shape
    L = k.shape[1]
    nb = -(-L // MOBA_BLOCK)
    n_sel = min(MOBA_TOPK, nb)
    S = n_sel + 1
    pad = nb * MOBA_BLOCK - L

    def to_blocks(a):
        a = jnp.pad(a, ((0, 0), (0, pad), (0, 0), (0, 0)))
        return a.reshape(B, nb, MOBA_BLOCK, H, Dh).transpose(0, 3, 1, 2, 4)

    kb, vb = to_blocks(k), to_blocks(v)
    k_mean = jnp.mean(kb.astype(f32), axis=3)
    qpos = q_start + jnp.arange(T)
    own = qpos // MOBA_BLOCK
    gate = jnp.einsum('bthd,bhnd->bhtn', q.astype(f32), k_mean)
    fully_past = jnp.arange(nb)[None, :] < own[:, None]
    gate = jnp.where(fully_past, gate, -jnp.inf)
    _, sel = lax.top_k(gate, n_sel)
    own_b = jnp.broadcast_to(own[None, None, :, None], (B, H, T, 1)).astype(sel.dtype)
    blocks = jnp.concatenate([sel, own_b], axis=-1)
    slot_ok = jnp.concatenate([jnp.arange(n_sel)[None, :] < jnp.minimum(own, n_sel)[:, None],
                               jnp.ones((T, 1), dtype=bool)], axis=-1)

    qc = math.gcd(T, MOBA_Q_CHUNK)
    nq = T // qc
    q_c = q.transpose(0, 2, 1, 3).reshape(B, H, nq, qc, Dh).transpose(2, 0, 1, 3, 4)
    b_c = blocks.reshape(B, H, nq, qc, S).transpose(2, 0, 1, 3, 4)
    pos_c = qpos.reshape(nq, qc)
    ok_c = slot_ok.reshape(nq, qc, S)
    bi = jnp.arange(B)[:, None, None, None]
    hi = jnp.arange(H)[None, :, None, None]
    offs = jnp.arange(MOBA_BLOCK)
    scale = Dh ** -0.5

    def attend(xs):
        qq, bb, pp, oo = xs
        kg = kb[bi, hi, bb].astype(f32)
        vg = vb[bi, hi, bb].astype(f32)
        s = jnp.einsum('bhqd,bhqsjd->bhqsj', qq.astype(f32), kg) * scale
        kpos = bb[..., None] * MOBA_BLOCK + offs
        mask = oo[None, None, :, :, None] & (kpos <= pp[None, None, :, None, None])
        s = jnp.where(mask, s, -jnp.inf).reshape(B, H, qc, S * MOBA_BLOCK)
        p = jax.nn.softmax(s, axis=-1).reshape(B, H, qc, S, MOBA_BLOCK)
        return jnp.einsum('bhqsj,bhqsjd->bhqd', p, vg).astype(q.dtype)

    out = lax.map(attend, (q_c, b_c, pos_c, ok_c))
    return out.transpose(1, 0, 3, 2, 4).reshape(B, T, H * Dh)


def token_mixers(u, lw, past_k, past_v, c0, n0, m0, conv0):
    B, T, _ = u.shape
    z = u @ lw['w_in']
    mq, mk, mv, mo, mi, mf, aq, ak, av, cb, cc, cx = split_cols(z)
    heads = lambda a: a.reshape(B, T, -1, HEAD_DIM)
    gates = jnp.concatenate([mi, mf], axis=-1).astype(jnp.float32) + lw['b_if'].astype(jnp.float32)
    h_m, c1, n1, m1 = mlstm_chunkwise(heads(mq), heads(mk), heads(mv), gates[..., :H_M], gates[..., H_M:], c0, n0, m0)
    h_m = rmsnorm(h_m, lw['h_norm_g'].reshape(H_M, HEAD_DIM)).reshape(B, T, D_M)
    h_m = (h_m.astype(jnp.float32) * jax.nn.sigmoid(mo.astype(jnp.float32))).astype(u.dtype)
    q = rmsnorm(heads(aq), lw['q_norm_g'])
    k_new = rmsnorm(heads(ak), lw['k_norm_g'])
    v_new = heads(av)
    if past_k is None:
        k_all, v_all, q_start = k_new, v_new, 0
    else:
        k_all = jnp.concatenate([past_k.astype(k_new.dtype), k_new], axis=1)
        v_all = jnp.concatenate([past_v.astype(v_new.dtype), v_new], axis=1)
        q_start = past_k.shape[1]
    h_a = moba_attention(q, k_all, v_all, q_start)
    ucv = cc * cx
    ext = jnp.concatenate([conv0.astype(ucv.dtype), ucv], axis=1)
    w = lw['conv_w']
    y = sum(w[j] * ext[:, j:j + T] for j in range(CONV_WIDTH))
    h_c = cb * y
    conv1 = ext[:, T:]
    out = jnp.concatenate([h_m, h_a, h_c], axis=-1) @ lw['w_out']
    return out, (k_new, v_new, c1, n1, m1, conv1)


def trunk_layer(h, p_l, lw, past_k, past_v, c0, n0, m0, conv0):
    h = h + 0.5 * swiglu(rmsnorm(h, lw['g_ffn1']), lw['w1_a'], lw['w3_a'], lw['w2_a'])
    mix, state = token_mixers(rmsnorm(h, lw['g_mix']), lw, past_k, past_v, c0, n0, m0, conv0)
    h = h + mix
    h = h + 0.5 * swiglu(rmsnorm(h, lw['g_ffn2']), lw['w1_b'], lw['w3_b'], lw['w2_b'])
    gate = jax.nn.sigmoid((rmsnorm(h, lw['g_ple']) @ lw['w_pgate']).astype(jnp.float32))
    h = h + (gate * (p_l @ lw['w_ple']).astype(jnp.float32)).astype(h.dtype)
    return h, state


def setup_inputs(seed: int = 0) -> dict:
    key = jax.random.key(seed)
    ks = jax.random.split(key, 40)
    f32 = jnp.float32
    nrm = lambda k, shape, scale=1.0: scale * jax.random.normal(k, shape, f32)
    n_pages = PAST_LEN // PAGE_SIZE
    n_used = DEC_BATCH * n_pages
    n_pool = n_used + max(1, n_used // 4)
    perm = jax.random.permutation(ks[0], n_pool)
    page_table = perm[:n_used].reshape(DEC_BATCH, n_pages).astype(jnp.int32)
    gain = lambda k, shape: 1.0 + nrm(k, shape, 0.05)
    b_i = nrm(ks[1], (DEPTH, H_M), 0.1)
    b_f = jnp.linspace(FORGET_BIAS_LO, FORGET_BIAS_HI, H_M, dtype=f32)[None, :] + nrm(ks[2], (DEPTH, H_M), 0.1)
    return {
        'x_prompt': nrm(ks[3], (BATCH, SEQ, D_MODEL)),
        'x_sample': nrm(ks[4], (DEC_BATCH, DEC_SEQ, D_MODEL)),
        'cache_k': nrm(ks[5], (DEPTH, n_pool, PAGE_SIZE, H_A, HEAD_DIM)),
        'cache_v': nrm(ks[6], (DEPTH, n_pool, PAGE_SIZE, H_A, HEAD_DIM)),
        'state_mlstm_c': nrm(ks[7], (DEPTH, DEC_BATCH, H_M, HEAD_DIM, HEAD_DIM), HEAD_DIM ** -0.5),
        'state_mlstm_n': nrm(ks[8], (DEPTH, DEC_BATCH, H_M, HEAD_DIM), HEAD_DIM ** -0.5),
        'state_mlstm_m': nrm(ks[9], (DEPTH, DEC_BATCH, H_M)),
        'state_conv': nrm(ks[10], (DEPTH, DEC_BATCH, CONV_WIDTH - 1, D_C)),
        'page_table': page_table,
        'p_prompt': nrm(ks[11], (DEPTH, BATCH, SEQ, PLE_DIM)),
        'p_sample': nrm(ks[12], (DEPTH, DEC_BATCH, DEC_SEQ, PLE_DIM)),
        'g_ffn1': gain(ks[13], (DEPTH, D_MODEL)),
        'w1_a': nrm(ks[14], (DEPTH, D_MODEL, D_FF), D_MODEL ** -0.5),
        'w3_a': nrm(ks[15], (DEPTH, D_MODEL, D_FF), D_MODEL ** -0.5),
        'w2_a': nrm(ks[16], (DEPTH, D_FF, D_MODEL), D_FF ** -0.5),
        'g_mix': gain(ks[17], (DEPTH, D_MODEL)),
        'w_in': nrm(ks[18], (DEPTH, D_MODEL, N_IN), D_MODEL ** -0.5),
        'b_if': jnp.concatenate([b_i, b_f], axis=-1),
        'q_norm_g': gain(ks[19], (DEPTH, HEAD_DIM)),
        'k_norm_g': gain(ks[20], (DEPTH, HEAD_DIM)),
        'h_norm_g': gain(ks[21], (DEPTH, D_M)),
        'conv_w': nrm(ks[22], (DEPTH, CONV_WIDTH, D_C), CONV_WIDTH ** -0.5),
        'w_out': nrm(ks[23], (DEPTH, D_MODEL, D_MODEL), D_MODEL ** -0.5),
        'g_ffn2': gain(ks[24], (DEPTH, D_MODEL)),
        'w1_b': nrm(ks[25], (DEPTH, D_MODEL, D_FF), D_MODEL ** -0.5),
        'w3_b': nrm(ks[26], (DEPTH, D_MODEL, D_FF), D_MODEL ** -0.5),
        'w2_b': nrm(ks[27], (DEPTH, D_FF, D_MODEL), D_FF ** -0.5),
        'g_ple': gain(ks[28], (DEPTH, D_MODEL)),
        'w_pgate': nrm(ks[29], (DEPTH, D_MODEL, D_MODEL), D_MODEL ** -0.5),
        'w_ple': nrm(ks[30], (DEPTH, PLE_DIM, D_MODEL), PLE_DIM ** -0.5),
    }


def reference(x_prompt, x_sample, cache_k, cache_v, state_mlstm_c, state_mlstm_n, state_mlstm_m, state_conv,
              page_table, p_prompt, p_sample, g_ffn1, w1_a, w3_a, w2_a, g_mix, w_in, b_if, q_norm_g, k_norm_g,
              h_norm_g, conv_w, w_out, g_ffn2, w1_b, w3_b, w2_b, g_ple, w_pgate, w_ple):
    f32 = jnp.float32
    bp = x_prompt.shape[0]
    bs, n_pages = page_table.shape
    past_len = n_pages * PAGE_SIZE
    hp, hs = x_prompt, x_sample
    kp, vp, ksm, vsm = [], [], [], []
    cp, npr, mp, csm, nsm, msm = [], [], [], [], [], []
    cvp, cvs = [], []
    for l in range(DEPTH):
        lw = dict(g_ffn1=g_ffn1[l], w1_a=w1_a[l], w3_a=w3_a[l], w2_a=w2_a[l], g_mix=g_mix[l], w_in=w_in[l],
                  b_if=b_if[l], q_norm_g=q_norm_g[l], k_norm_g=k_norm_g[l], h_norm_g=h_norm_g[l],
                  conv_w=conv_w[l], w_out=w_out[l], g_ffn2=g_ffn2[l], w1_b=w1_b[l], w3_b=w3_b[l],
                  w2_b=w2_b[l], g_ple=g_ple[l], w_pgate=w_pgate[l], w_ple=w_ple[l])
        hp, st_p = trunk_layer(hp, p_prompt[l], lw, None, None,
                               jnp.zeros((bp, H_M, HEAD_DIM, HEAD_DIM), f32),
                               jnp.zeros((bp, H_M, HEAD_DIM), f32),
                               jnp.zeros((bp, H_M), f32),
                               jnp.zeros((bp, CONV_WIDTH - 1, D_C), x_prompt.dtype))
        past_k = cache_k[l][page_table].reshape(bs, past_len, H_A, HEAD_DIM)
        past_v = cache_v[l][page_table].reshape(bs, past_len, H_A, HEAD_DIM)
        hs, st_s = trunk_layer(hs, p_sample[l], lw, past_k, past_v, state_mlstm_c[l], state_mlstm_n[l],
                               state_mlstm_m[l], state_conv[l])
        kp.append(st_p[0]); vp.append(st_p[1]); cp.append(st_p[2]); npr.append(st_p[3]); mp.append(st_p[4]); cvp.append(st_p[5])
        ksm.append(st_s[0]); vsm.append(st_s[1]); csm.append(st_s[2]); nsm.append(st_s[3]); msm.append(st_s[4]); cvs.append(st_s[5])
    return (hp, hs, jnp.stack(kp), jnp.stack(vp), jnp.stack(ksm), jnp.stack(vsm),
            jnp.stack(cp), jnp.stack(npr), jnp.stack(mp), jnp.stack(csm), jnp.stack(nsm), jnp.stack(msm),
            jnp.stack(cvp), jnp.stack(cvs))
```

```python
import functools
import math

import jax
import jax.numpy as jnp
from jax import lax
from jax.experimental import pallas as pl
from jax.experimental.pallas import tpu as pltpu

F32 = jnp.float32
BF16 = jnp.bfloat16

HEAD_DIM = 128
PAGE_SIZE = 128
MOBA_BLOCK = 256
MOBA_TOPK = 3
RMS_EPS = 1e-6
NEG = -1e30
MLSTM_CHUNK = 256
V7X_VMEM_LIMIT = 56 * 1024 * 1024


def _cparams(sem, vmem=V7X_VMEM_LIMIT):
    return pltpu.CompilerParams(dimension_semantics=sem, vmem_limit_bytes=vmem)


def _split3(x):
    hi = x.astype(BF16)
    r1 = x - hi.astype(F32)
    mid = r1.astype(BF16)
    lo = (r1 - mid.astype(F32)).astype(BF16)
    return hi, mid, lo


def _dot_nn(a, b):
    return jnp.dot(a, b, preferred_element_type=F32)


def _dot_nt(a, b):
    return lax.dot_general(a, b, (((1,), (1,)), ((), ())), preferred_element_type=F32)


def _dot_tn(a, b):
    return lax.dot_general(a, b, (((0,), (0,)), ((), ())), preferred_element_type=F32)


def _dot_hp(a, b, dot):
    a3, b3 = _split3(a), _split3(b)
    out = None
    for i, j in ((2, 0), (0, 2), (1, 1), (1, 0), (0, 1), (0, 0)):
        t = dot(a3[i], b3[j])
        out = t if out is None else out + t
    return out


def _dot_hp_exact_lhs(a_bf16, b, dot):
    b3 = _split3(b)
    return dot(a_bf16, b3[2]) + dot(a_bf16, b3[1]) + dot(a_bf16, b3[0])


def _dot_hp_exact_rhs(a, b_bf16, dot):
    a3 = _split3(a)
    return dot(a3[2], b_bf16) + dot(a3[1], b_bf16) + dot(a3[0], b_bf16)


def _rms_rows(x, g):
    return x * lax.rsqrt(jnp.mean(x * x, axis=-1, keepdims=True) + RMS_EPS) * g


def _log_sigmoid(x):
    return jnp.minimum(x, 0.0) - jnp.log1p(jnp.exp(-jnp.abs(x)))


def _sigmoid(x):
    return 1.0 / (1.0 + jnp.exp(-x))


def _rmsnorm_kernel(x_ref, g_ref, o_ref):
    o_ref[...] = _rms_rows(x_ref[...], g_ref[...]).astype(o_ref.dtype)


def _rmsnorm_gates_kernel(x_ref, g_ref, wg_ref, bg_ref, o_ref, gates_ref):
    y = _rms_rows(x_ref[...], g_ref[...])
    o_ref[...] = y.astype(o_ref.dtype)
    gates_ref[...] = _dot_hp(y, wg_ref[...], _dot_nn) + bg_ref[...]


def rmsnorm_bf16(x, g, *, tm=512):
    m, d = x.shape
    return pl.pallas_call(
        _rmsnorm_kernel,
        grid=(pl.cdiv(m, tm),),
        in_specs=[pl.BlockSpec((tm, d), lambda i: (i, 0)),
                  pl.BlockSpec((1, d), lambda i: (0, 0))],
        out_specs=pl.BlockSpec((tm, d), lambda i: (i, 0)),
        out_shape=jax.ShapeDtypeStruct((m, d), BF16),
        compiler_params=_cparams(("parallel",)),
        name="rmsnorm",
    )(x, g.reshape(1, d))


def rmsnorm_gates(x, g, wg, bg, *, tm=512):
    m, d = x.shape
    ng = wg.shape[1]
    return pl.pallas_call(
        _rmsnorm_gates_kernel,
        grid=(pl.cdiv(m, tm),),
        in_specs=[pl.BlockSpec((tm, d), lambda i: (i, 0)),
                  pl.BlockSpec((1, d), lambda i: (0, 0)),
                  pl.BlockSpec((d, ng), lambda i: (0, 0)),
                  pl.BlockSpec((1, ng), lambda i: (0, 0))],
        out_specs=[pl.BlockSpec((tm, d), lambda i: (i, 0)),
                   pl.BlockSpec((tm, ng), lambda i: (i, 0))],
        out_shape=[jax.ShapeDtypeStruct((m, d), BF16),
                   jax.ShapeDtypeStruct((m, ng), F32)],
        compiler_params=_cparams(("parallel",)),
        name="rmsnorm_gates",
    )(x, g.reshape(1, d), wg, bg)


def _row_cases(m, tm, body):
    nm = pl.cdiv(m, tm)
    rem = m - (nm - 1) * tm
    if rem == tm:
        body(tm)
        return
    i = pl.program_id(1)

    @pl.when(i < nm - 1)
    def _():
        body(tm)

    @pl.when(i == nm - 1)
    def _():
        body(rem)


def _gate_up_kernel(a_ref, w1_ref, w3_ref, o_ref, *, m, tm):
    def body(rows):
        a = a_ref[:rows, :]
        x1 = jnp.dot(a, w1_ref[...], preferred_element_type=F32)
        x3 = jnp.dot(a, w3_ref[...], preferred_element_type=F32)
        o_ref[:rows, :] = (x1 * _sigmoid(x1) * x3).astype(o_ref.dtype)
    _row_cases(m, tm, body)


def gate_up(a, w1, w3, *, tm=1024, tn=512):
    m, k = a.shape
    n = w1.shape[1]
    return pl.pallas_call(
        functools.partial(_gate_up_kernel, m=m, tm=tm),
        grid=(pl.cdiv(n, tn), pl.cdiv(m, tm)),
        in_specs=[pl.BlockSpec((tm, k), lambda j, i: (i, 0)),
                  pl.BlockSpec((k, tn), lambda j, i: (0, j)),
                  pl.BlockSpec((k, tn), lambda j, i: (0, j))],
        out_specs=pl.BlockSpec((tm, tn), lambda j, i: (i, j)),
        out_shape=jax.ShapeDtypeStruct((m, n), BF16),
        compiler_params=_cparams(("parallel", "arbitrary")),
        name="gate_up",
    )(a, w1, w3)


def _mm_resid_kernel(a_ref, w_ref, h_ref, o_ref, *, m, tm, alpha):
    def body(rows):
        acc = jnp.dot(a_ref[:rows, :], w_ref[...], preferred_element_type=F32)
        o_ref[:rows, :] = h_ref[:rows, :] + alpha * acc
    _row_cases(m, tm, body)


def mm_resid(a, w, h, *, alpha, tm, tn):
    m, k = a.shape
    n = w.shape[1]
    return pl.pallas_call(
        functools.partial(_mm_resid_kernel, m=m, tm=tm, alpha=alpha),
        grid=(pl.cdiv(n, tn), pl.cdiv(m, tm)),
        in_specs=[pl.BlockSpec((tm, k), lambda j, i: (i, 0)),
                  pl.BlockSpec((k, tn), lambda j, i: (0, j)),
                  pl.BlockSpec((tm, tn), lambda j, i: (i, j))],
        out_specs=pl.BlockSpec((tm, tn), lambda j, i: (i, j)),
        out_shape=jax.ShapeDtypeStruct((m, n), F32),
        compiler_params=_cparams(("parallel", "arbitrary")),
        name="mm_resid",
    )(a, w, h)


def _mm_kernel(a_ref, w_ref, o_ref, *, m, tm):
    def body(rows):
        o_ref[:rows, :] = jnp.dot(a_ref[:rows, :], w_ref[...], preferred_element_type=F32)
    _row_cases(m, tm, body)


def mm(a, w, *, tm, tn):
    m, k = a.shape
    n = w.shape[1]
    return pl.pallas_call(
        functools.partial(_mm_kernel, m=m, tm=tm),
        grid=(pl.cdiv(n, tn), pl.cdiv(m, tm)),
        in_specs=[pl.BlockSpec((tm, k), lambda j, i: (i, 0)),
                  pl.BlockSpec((k, tn), lambda j, i: (0, j))],
        out_specs=pl.BlockSpec((tm, tn), lambda j, i: (i, j)),
        out_shape=jax.ShapeDtypeStruct((m, n), F32),
        compiler_params=_cparams(("parallel", "arbitrary")),
        name="in_proj",
    )(a, w)


def _ple_kernel(a_ref, wg_ref, p_ref, wp_ref, h_ref, o_ref, *, m, tm):
    def body(rows):
        gate = _sigmoid(jnp.dot(a_ref[:rows, :], wg_ref[...], preferred_element_type=F32))
        pe = jnp.dot(p_ref[:rows, :].astype(BF16), wp_ref[...], preferred_element_type=F32)
        o_ref[:rows, :] = h_ref[:rows, :] + gate * pe
    _row_cases(m, tm, body)


def ple(a, wg, p, wp, h, *, tm=1024, tn=512):
    m, k = a.shape
    n = wg.shape[1]
    kp = p.shape[1]
    return pl.pallas_call(
        functools.partial(_ple_kernel, m=m, tm=tm),
        grid=(pl.cdiv(n, tn), pl.cdiv(m, tm)),
        in_specs=[pl.BlockSpec((tm, k), lambda j, i: (i, 0)),
                  pl.BlockSpec((k, tn), lambda j, i: (0, j)),
                  pl.BlockSpec((tm, kp), lambda j, i: (i, 0)),
                  pl.BlockSpec((kp, tn), lambda j, i: (0, j)),
                  pl.BlockSpec((tm, tn), lambda j, i: (i, j))],
        out_specs=pl.BlockSpec((tm, tn), lambda j, i: (i, j)),
        out_shape=jax.ShapeDtypeStruct((m, n), F32),
        compiler_params=_cparams(("parallel", "arbitrary")),
        name="ple",
    )(a, wg, p, wp, h)


def _mlstm_kernel(zq_ref, zk_ref, zv_ref, zo_ref, gc_ref, gr_ref, c0_ref, n0_ref, m0_ref, hg_ref,
                  hm_ref, c1_ref, n1_ref, m1_ref, ct_sc, n_sc, m_sc, *, n_heads, chunk):
    c = pl.program_id(1)
    last = pl.num_programs(1) - 1
    L = chunk
    scale = HEAD_DIM ** -0.5

    @pl.when(c == 0)
    def _():
        for h in range(n_heads):
            ct_sc[h] = c0_ref[h].T
        n_sc[...] = n0_ref[...]
        m_sc[...] = m0_ref[...]

    row = lax.broadcasted_iota(jnp.int32, (L, L), 0)
    col = lax.broadcasted_iota(jnp.int32, (L, L), 1)
    causal = col <= row
    tri_lo = jnp.where(causal, 1.0, 0.0).astype(BF16)
    tri_up = jnp.where(row <= col, 1.0, 0.0).astype(BF16)

    gc = gc_ref[...]
    gr = gr_ref[...]
    f_cols = _dot_hp_exact_lhs(tri_lo, _log_sigmoid(gc), _dot_nn)
    f_rows = _dot_hp_exact_rhs(_log_sigmoid(gr), tri_up, _dot_nn)

    for h in range(n_heads):
        lanes = slice(h * HEAD_DIM, (h + 1) * HEAD_DIM)
        q = zq_ref[:, lanes]
        k = zk_ref[:, lanes] * scale
        v = zv_ref[:, lanes]
        fc = f_cols[:, n_heads + h:n_heads + h + 1]
        fr = f_rows[n_heads + h:n_heads + h + 1, :]
        ic = gc[:, h:h + 1]
        ir = gr[h:h + 1, :]
        m_prev = m_sc[h:h + 1, 0:1]
        n_prev = n_sc[h:h + 1, :]
        ct_prev = ct_sc[h]

        dmat = jnp.where(causal, fc - fr + ir, NEG)
        inter = fc + m_prev
        m_t = jnp.maximum(inter, jnp.max(dmat, axis=-1, keepdims=True))
        w_intra = jnp.exp(dmat - m_t)
        w_inter = jnp.exp(inter - m_t)
        qb, kb, vb = q.astype(BF16), k.astype(BF16), v.astype(BF16)
        a = w_intra * _dot_nt(qb, kb)
        num = (w_inter * jnp.dot(qb, ct_prev.astype(BF16), preferred_element_type=F32)
               + jnp.dot(a.astype(BF16), vb, preferred_element_type=F32))
        den = w_inter * jnp.sum(q * n_prev, axis=-1, keepdims=True) + jnp.sum(a, axis=-1, keepdims=True)
        hval = num / jnp.maximum(jnp.abs(den), jnp.exp(-m_t))

        m_end = m_t[L - 1:L, :]
        w_s = jnp.exp(fc[L - 1:L, :] - fc + ic - m_end)
        decay = jnp.exp(inter[L - 1:L, :] - m_end)
        kw = w_s * k
        ct_new = decay * ct_prev + _dot_tn(kw.astype(BF16), vb)
        n_new = decay * n_prev + jnp.sum(kw, axis=0, keepdims=True)
        ct_sc[h] = ct_new
        n_sc[h:h + 1, :] = n_new
        m_sc[h:h + 1, :] = jnp.broadcast_to(m_end, (1, HEAD_DIM))

        hn = _rms_rows(hval, hg_ref[:, lanes])
        hm_ref[:, lanes] = (hn * _sigmoid(zo_ref[:, lanes])).astype(hm_ref.dtype)

    @pl.when(c == last)
    def _():
        for h in range(n_heads):
            c1_ref[h] = ct_sc[h].T
        n1_ref[...] = n_sc[...]
        m1_ref[...] = m_sc[...]


def mlstm(zz, gates_c, gates_r, c0, n0, m0, h_norm_g, *, batch, seq, chunk):
    n_heads = c0.shape[1]
    dm = n_heads * HEAD_DIM
    nc = seq // chunk
    m0b = jnp.broadcast_to(m0[:, :, None], (batch, n_heads, HEAD_DIM))
    zspec = lambda g: pl.BlockSpec((chunk, dm), lambda b, c, g=g: (b * nc + c, g))
    state = lambda *shape: pl.BlockSpec((None,) + shape, lambda b, c: (b,) + (0,) * len(shape))
    hm, c1, n1, m1 = pl.pallas_call(
        functools.partial(_mlstm_kernel, n_heads=n_heads, chunk=chunk),
        grid=(batch, nc),
        in_specs=[zspec(0), zspec(1), zspec(2), zspec(3),
                  pl.BlockSpec((chunk, gates_c.shape[1]), lambda b, c: (b * nc + c, 0)),
                  pl.BlockSpec((None, 2 * n_heads, chunk), lambda b, c: (b, 0, c)),
                  state(n_heads, HEAD_DIM, HEAD_DIM), state(n_heads, HEAD_DIM), state(n_heads, HEAD_DIM),
                  pl.BlockSpec((1, dm), lambda b, c: (0, 0))],
        out_specs=[pl.BlockSpec((chunk, dm), lambda b, c: (b * nc + c, 0)),
                   state(n_heads, HEAD_DIM, HEAD_DIM), state(n_heads, HEAD_DIM), state(n_heads, HEAD_DIM)],
        out_shape=[jax.ShapeDtypeStruct((batch * seq, dm), BF16),
                   jax.ShapeDtypeStruct((batch, n_heads, HEAD_DIM, HEAD_DIM), F32),
                   jax.ShapeDtypeStruct((batch, n_heads, HEAD_DIM), F32),
                   jax.ShapeDtypeStruct((batch, n_heads, HEAD_DIM), F32)],
        scratch_shapes=[pltpu.VMEM((n_heads, HEAD_DIM, HEAD_DIM), F32),
                        pltpu.VMEM((n_heads, HEAD_DIM), F32),
                        pltpu.VMEM((n_heads, HEAD_DIM), F32)],
        compiler_params=_cparams(("parallel", "arbitrary")),
        name="mlstm",
    )(zz, zz, zz, zz, gates_c, gates_r, c0, n0, m0b, h_norm_g.reshape(1, dm))
    return hm, c1, n1, m1[:, :, 0]


def _conv_kernel(cb_ref, cc_ref, cx_ref, s0_ref, w_ref, hc_ref, s1_ref, carry_sc, *, rows, last_row):
    t = pl.program_id(1)
    last = pl.num_programs(1) - 1

    @pl.when(t == 0)
    def _():
        carry_sc[...] = s0_ref[...]

    u = cc_ref[...] * cx_ref[...]
    prev2 = carry_sc[0:1, :]
    prev1 = carry_sc[1:2, :]
    r = lax.broadcasted_iota(jnp.int32, u.shape, 0)
    u1 = jnp.where(r == 0, prev1, pltpu.roll(u, 1, 0))
    u2 = jnp.where(r == 0, prev2, jnp.where(r == 1, prev1, pltpu.roll(u, 2, 0)))
    y = w_ref[0:1, :] * u2 + w_ref[1:2, :] * u1 + w_ref[2:3, :] * u
    hc_ref[...] = (cb_ref[...] * y).astype(hc_ref.dtype)
    carry_sc[...] = u[rows - 2:rows, :]

    @pl.when(t == last)
    def _():
        s1_ref[...] = u[last_row - 1:last_row + 1, :]


def short_conv(zz, col0, s0, w, *, batch, seq, rows, valid):
    c = s0.shape[-1]
    nt = seq // rows
    last_row = valid - 1 - (nt - 1) * rows
    zspec = lambda g: pl.BlockSpec((rows, c), lambda b, t, g=g: (b * nt + t, col0 + g))
    return pl.pallas_call(
        functools.partial(_conv_kernel, rows=rows, last_row=last_row),
        grid=(batch, nt),
        in_specs=[zspec(0), zspec(1), zspec(2),
                  pl.BlockSpec((None, 2, c), lambda b, t: (b, 0, 0)),
                  pl.BlockSpec((3, c), lambda b, t: (0, 0))],
        out_specs=[pl.BlockSpec((rows, c), lambda b, t: (b * nt + t, 0)),
                   pl.BlockSpec((None, 2, c), lambda b, t: (b, 0, 0))],
        out_shape=[jax.ShapeDtypeStruct((batch * seq, c), BF16),
                   jax.ShapeDtypeStruct((batch, 2, c), F32)],
        scratch_shapes=[pltpu.VMEM((2, c), F32)],
        compiler_params=_cparams(("parallel", "arbitrary")),
        name="short_conv",
    )(zz, zz, zz, s0, w)


def _top_mask_rows(gate, valid, n_keep):
    n = gate.shape[0]
    jidx = lax.broadcasted_iota(jnp.int32, gate.shape, 0)
    g = jnp.where(valid, gate, NEG)
    rank = jnp.zeros(gate.shape, F32)
    for j in range(n):
        gj = g[j:j + 1, :]
        beats = (gj > g) | ((gj == g) & (jidx > j))
        rank = rank + jnp.where(beats, 1.0, 0.0)
    return jnp.where(valid & (rank < n_keep), 1.0, 0.0)


def _moba_prompt_kernel(q_ref, k_ref, v_ref, gq_ref, gk_ref, ko_ref, vo_ref, ha_ref, *, seq):
    nb = seq // MOBA_BLOCK
    scale = HEAD_DIM ** -0.5
    qn = _rms_rows(q_ref[...], gq_ref[...])
    kn = _rms_rows(k_ref[...], gk_ref[...])
    v = v_ref[...]
    ko_ref[...] = kn
    vo_ref[...] = v
    qb, kb, vb = qn.astype(BF16), kn.astype(BF16), v.astype(BF16)

    avg = jnp.where(lax.broadcasted_iota(jnp.int32, (HEAD_DIM, seq), 1) // MOBA_BLOCK
                    == lax.broadcasted_iota(jnp.int32, (HEAD_DIM, seq), 0), 1.0 / MOBA_BLOCK, 0.0).astype(BF16)
    k_mean = _dot_hp_exact_lhs(avg, kn, _dot_nn)
    gate_t = _dot_hp(k_mean, qn, _dot_nt)
    own = lax.broadcasted_iota(jnp.int32, gate_t.shape, 1) // MOBA_BLOCK
    blk = lax.broadcasted_iota(jnp.int32, gate_t.shape, 0)
    sel = _top_mask_rows(gate_t[:8 * pl.cdiv(nb, 8)], (blk < own)[:8 * pl.cdiv(nb, 8)], MOBA_TOPK)
    sel = jnp.concatenate([sel, jnp.zeros((HEAD_DIM - sel.shape[0], seq), F32)], axis=0).T

    r = lax.broadcasted_iota(jnp.int32, (MOBA_BLOCK, MOBA_BLOCK), 0)
    c = lax.broadcasted_iota(jnp.int32, (MOBA_BLOCK, MOBA_BLOCK), 1)
    for i in range(nb):
        rows = slice(i * MOBA_BLOCK, (i + 1) * MOBA_BLOCK)
        kv = (i + 1) * MOBA_BLOCK
        s = _dot_nt(qb[rows], kb[:kv]) * scale
        parts = [jnp.where(sel[rows, j:j + 1] > 0.5, s[:, j * MOBA_BLOCK:(j + 1) * MOBA_BLOCK], NEG)
                 for j in range(i)]
        parts.append(jnp.where(c <= r, s[:, i * MOBA_BLOCK:], NEG))
        s = parts[0] if i == 0 else jnp.concatenate(parts, axis=1)
        p = jnp.exp(s - jnp.max(s, axis=-1, keepdims=True))
        o = jnp.dot(p.astype(BF16), vb[:kv], preferred_element_type=F32) / jnp.sum(p, axis=-1, keepdims=True)
        ha_ref[rows, :] = o.astype(ha_ref.dtype)


def moba_prompt(zz, colq, gq, gk, *, batch, seq, n_heads):
    d = HEAD_DIM
    zspec = lambda g: pl.BlockSpec((seq, d), lambda b, h, g=g: (b, colq + g * n_heads + h))
    ospec = pl.BlockSpec((None, seq, d), lambda b, h: (b, 0, h))
    ko, vo, ha = pl.pallas_call(
        functools.partial(_moba_prompt_kernel, seq=seq),
        grid=(batch, n_heads),
        in_specs=[zspec(0), zspec(1), zspec(2),
                  pl.BlockSpec((1, d), lambda b, h: (0, 0)),
                  pl.BlockSpec((1, d), lambda b, h: (0, 0))],
        out_specs=[ospec, ospec, pl.BlockSpec((seq, d), lambda b, h: (b, h))],
        out_shape=[jax.ShapeDtypeStruct((batch, seq, n_heads * d), F32),
                   jax.ShapeDtypeStruct((batch, seq, n_heads * d), F32),
                   jax.ShapeDtypeStruct((batch * seq, n_heads * d), BF16)],
        compiler_params=_cparams(("parallel", "parallel")),
        name="moba_prompt",
    )(zz, zz, zz, gq.reshape(1, d), gk.reshape(1, d))
    return ha, ko, vo


TPAD = 8


def _moba_pages_kernel(pt_ref, q_ref, k0_ref, k1_ref, v0_ref, v1_ref, gq_ref,
                       km_ref, ms_ref, ls_ref, op_ref, q_sc, *, n_heads):
    del pt_ref
    j = pl.program_id(1)
    scale = HEAD_DIM ** -0.5
    rows = n_heads * TPAD

    @pl.when(j == 0)
    def _():
        for h in range(n_heads):
            qh = _rms_rows(q_ref[:, h * HEAD_DIM:(h + 1) * HEAD_DIM], gq_ref[...])
            q_sc[h * TPAD:(h + 1) * TPAD, :] = qh.astype(BF16)
        ms_ref[...] = jnp.zeros(ms_ref.shape, F32)
        ls_ref[...] = jnp.zeros(ls_ref.shape, F32)

    k0, k1 = k0_ref[...], k1_ref[...]
    km_ref[...] = (jnp.sum(k0, axis=0) + jnp.sum(k1, axis=0)) * (1.0 / MOBA_BLOCK)
    flat = lambda x: x.reshape(PAGE_SIZE * n_heads, HEAD_DIM).astype(BF16)
    kb = jnp.concatenate([flat(k0), flat(k1)], axis=0)
    vb = jnp.concatenate([flat(v0_ref[...]), flat(v1_ref[...])], axis=0)
    s = _dot_nt(q_sc[...], kb) * scale
    qhead = lax.broadcasted_iota(jnp.int32, s.shape, 0) // TPAD
    khead = lax.broadcasted_iota(jnp.int32, s.shape, 1) % n_heads
    s = jnp.where(qhead == khead, s, NEG)
    m = jnp.max(s, axis=-1, keepdims=True)
    p = jnp.exp(s - m)
    l = jnp.sum(p, axis=-1, keepdims=True)
    op_ref[...] = jnp.dot(p.astype(BF16), vb, preferred_element_type=F32)
    lane = lax.broadcasted_iota(jnp.int32, (rows, ms_ref.shape[-1]), 1)
    ms_ref[...] = jnp.where(lane == j, m, ms_ref[...])
    ls_ref[...] = jnp.where(lane == j, l, ls_ref[...])


def moba_pages(page_table, q_pad, cache_k, cache_v, layer, gq, *, n_heads):
    batch, n_pages = page_table.shape
    nblk = n_pages * PAGE_SIZE // MOBA_BLOCK
    pages_per_blk = MOBA_BLOCK // PAGE_SIZE
    assert pages_per_blk == 2
    d = HEAD_DIM
    rows = n_heads * TPAD
    stat_lanes = 128 * pl.cdiv(nblk, 128)
    page = lambda o: pl.BlockSpec((None, None, PAGE_SIZE, n_heads, d),
                                  lambda b, j, pt, o=o: (layer, pt[b * n_pages + 2 * j + o], 0, 0, 0))
    grid_spec = pltpu.PrefetchScalarGridSpec(
        num_scalar_prefetch=1,
        grid=(batch, nblk),
        in_specs=[pl.BlockSpec((None, TPAD, n_heads * d), lambda b, j, pt: (b, 0, 0)),
                  page(0), page(1), page(0), page(1),
                  pl.BlockSpec((1, d), lambda b, j, pt: (0, 0))],
        out_specs=[pl.BlockSpec((None, None, n_heads, d), lambda b, j, pt: (b, j, 0, 0)),
                   pl.BlockSpec((None, rows, stat_lanes), lambda b, j, pt: (b, 0, 0)),
                   pl.BlockSpec((None, rows, stat_lanes), lambda b, j, pt: (b, 0, 0)),
                   pl.BlockSpec((None, None, rows, d), lambda b, j, pt: (b, j, 0, 0))],
        scratch_shapes=[pltpu.VMEM((rows, d), BF16)],
    )
    return pl.pallas_call(
        functools.partial(_moba_pages_kernel, n_heads=n_heads),
        grid_spec=grid_spec,
        out_shape=[jax.ShapeDtypeStruct((batch, nblk, n_heads, d), F32),
                   jax.ShapeDtypeStruct((batch, rows, stat_lanes), F32),
                   jax.ShapeDtypeStruct((batch, rows, stat_lanes), F32),
                   jax.ShapeDtypeStruct((batch, nblk, rows, d), F32)],
        compiler_params=_cparams(("parallel", "arbitrary")),
        name="moba_pages",
    )(page_table.reshape(-1), q_pad, cache_k, cache_k, cache_v, cache_v, gq.reshape(1, d))


def _moba_combine_kernel(q_ref, kn_ref, vn_ref, km_ref, ms_ref, ls_ref, op_ref, gq_ref, gk_ref,
                         ha_ref, ko_ref, *, n_heads, nblk, t_new):
    scale = HEAD_DIM ** -0.5
    rows = n_heads * TPAD
    gates, m_own, p_own, qs, vs = [], [], [], [], []
    trow = lax.broadcasted_iota(jnp.int32, (TPAD, 1), 0)
    for h in range(n_heads):
        lanes = slice(h * HEAD_DIM, (h + 1) * HEAD_DIM)
        qh = _rms_rows(q_ref[:, lanes], gq_ref[...])
        kh = _rms_rows(kn_ref[:, lanes], gk_ref[...])
        ko_ref[:, lanes] = kh
        gates.append(_dot_hp(qh, km_ref[:, h, :], _dot_nt))
        sc = [jnp.where(trow >= s_, jnp.sum(qh * kh[s_:s_ + 1, :], axis=-1, keepdims=True) * scale, NEG)
              for s_ in range(t_new)]
        mo = functools.reduce(jnp.maximum, sc)
        m_own.append(mo)
        p_own.append(sc)
        vs.append(vn_ref[:, lanes])
    gate = jnp.concatenate(gates, axis=0)
    m_own = jnp.concatenate(m_own, axis=0)

    lane = lax.broadcasted_iota(jnp.int32, gate.shape, 1)
    rank = jnp.zeros(gate.shape, F32)
    for j in range(nblk):
        gj = gate[:, j:j + 1]
        beats = (gj > gate) | ((gj == gate) & (lane > j))
        rank = rank + jnp.where(beats, 1.0, 0.0)
    sel = rank < min(MOBA_TOPK, nblk)

    ms = ms_ref[:, :nblk]
    ls = ls_ref[:, :nblk]
    m_star = jnp.maximum(jnp.max(jnp.where(sel, ms, NEG), axis=-1, keepdims=True), m_own)
    w = jnp.where(sel, jnp.exp(ms - m_star), 0.0)
    l = jnp.sum(w * ls, axis=-1, keepdims=True)
    o = jnp.zeros((rows, HEAD_DIM), F32)
    for j in range(nblk):
        o = o + w[:, j:j + 1] * op_ref[j]
    for h in range(n_heads):
        rs = slice(h * TPAD, (h + 1) * TPAD)
        oh, lh = o[rs], l[rs]
        for s_ in range(t_new):
            pw = jnp.exp(p_own[h][s_] - m_star[rs])
            lh = lh + pw
            oh = oh + pw * vs[h][s_:s_ + 1, :]
        ha_ref[:, h * HEAD_DIM:(h + 1) * HEAD_DIM] = (oh / lh).astype(ha_ref.dtype)


def moba_combine(q_pad, kn_pad, vn_pad, k_mean, ms, ls, op, gq, gk, *, n_heads, t_new):
    batch, nblk = k_mean.shape[:2]
    d = HEAD_DIM
    rows = n_heads * TPAD
    full = lambda a: pl.BlockSpec((None,) + a.shape[1:], lambda b: (b,) + (0,) * (a.ndim - 1))
    vec = pl.BlockSpec((1, d), lambda b: (0, 0))
    return pl.pallas_call(
        functools.partial(_moba_combine_kernel, n_heads=n_heads, nblk=nblk, t_new=t_new),
        grid=(batch,),
        in_specs=[full(q_pad), full(kn_pad), full(vn_pad), full(k_mean), full(ms), full(ls), full(op), vec, vec],
        out_specs=[pl.BlockSpec((None, TPAD, n_heads * d), lambda b: (b, 0, 0)),
                   pl.BlockSpec((None, TPAD, n_heads * d), lambda b: (b, 0, 0))],
        out_shape=[jax.ShapeDtypeStruct((batch, TPAD, n_heads * d), BF16),
                   jax.ShapeDtypeStruct((batch, TPAD, n_heads * d), F32)],
        compiler_params=_cparams(("parallel",)),
        name="moba_combine",
    )(q_pad, kn_pad, vn_pad, k_mean, ms, ls, op, gq.reshape(1, d), gk.reshape(1, d))


def _pad_rows(x, rows):
    return jnp.pad(x, ((0, 0), (0, rows - x.shape[1]), (0, 0)))


def kernel(x_prompt, x_sample, cache_k, cache_v, state_mlstm_c, state_mlstm_n, state_mlstm_m, state_conv,
           page_table, p_prompt, p_sample, g_ffn1, w1_a, w3_a, w2_a, g_mix, w_in, b_if, q_norm_g, k_norm_g,
           h_norm_g, conv_w, w_out, g_ffn2, w1_b, w3_b, w2_b, g_ple, w_pgate, w_ple):
    bp, seq, d_model = x_prompt.shape
    bs, t_new, _ = x_sample.shape
    depth = g_ffn1.shape[0]
    h_m = state_mlstm_c.shape[2]
    h_a = cache_k.shape[3]
    d_m, d_a = h_m * HEAD_DIM, h_a * HEAD_DIM
    d_c = d_model - d_m - d_a
    n_p, n_s = bp * seq, bs * t_new
    assert t_new <= TPAD and seq % MLSTM_CHUNK == 0 and seq % MOBA_BLOCK == 0

    g0 = 4 * d_m
    g1 = g0 + 2 * h_m
    n_gate_lanes = 128
    col_aq = g0 // HEAD_DIM
    col_cb = (g0 + 3 * d_a) // d_c

    h = jnp.concatenate([x_prompt.reshape(n_p, d_model), x_sample.reshape(n_s, d_model)], axis=0)
    outs = {k: [] for k in ("kp", "vp", "ks", "vs", "cp", "np", "mp", "cs", "ns", "ms", "cvp", "cvs")}
    s_chunk = 128

    for l in range(depth):
        bf = lambda w: w[l].astype(BF16)
        g = gate_up(rmsnorm_bf16(h, g_ffn1[l]), bf(w1_a), bf(w3_a))
        h = mm_resid(g, bf(w2_a), h, alpha=0.5, tm=256, tn=512)

        w_main = jnp.concatenate([w_in[l][:, :g0], w_in[l][:, g1:]], axis=1).astype(BF16)
        w_gate = jnp.pad(w_in[l][:, g0:g1], ((0, 0), (0, n_gate_lanes - 2 * h_m)))
        b_gate = jnp.pad(b_if[l], (0, n_gate_lanes - 2 * h_m)).reshape(1, n_gate_lanes)
        u, gates = rmsnorm_gates(h, g_mix[l], w_gate, b_gate)
        z = mm(u, w_main, tm=1024, tn=512)

        gr_p = gates[:n_p, :2 * h_m].reshape(bp, seq, 2 * h_m).transpose(0, 2, 1)
        hm_p, c_p, nn_p, m_p = mlstm(
            z, gates, gr_p, jnp.zeros((bp, h_m, HEAD_DIM, HEAD_DIM), F32), jnp.zeros((bp, h_m, HEAD_DIM), F32),
            jnp.zeros((bp, h_m), F32), h_norm_g[l], batch=bp, seq=seq, chunk=MLSTM_CHUNK)
        ha_p, k_p, v_p = moba_prompt(z, col_aq, q_norm_g[l], k_norm_g[l], batch=bp, seq=seq, n_heads=h_a)
        hc_p, cv_p = short_conv(z, col_cb, jnp.zeros((bp, 2, d_c), F32), conv_w[l],
                                batch=bp, seq=seq, rows=512, valid=seq)

        zs = z[n_p:].reshape(bs, t_new, -1)
        gs = gates[n_p:].reshape(bs, t_new, n_gate_lanes)
        pad_gate = jnp.concatenate([jnp.full((h_m,), NEG, F32), jnp.full((h_m,), -NEG, F32),
                                    jnp.zeros((n_gate_lanes - 2 * h_m,), F32)])
        gs_pad = jnp.concatenate([gs, jnp.broadcast_to(pad_gate, (bs, s_chunk - t_new, n_gate_lanes))], axis=1)
        hm_s, c_s, nn_s, m_s = mlstm(
            _pad_rows(zs[:, :, :g0], s_chunk).reshape(bs * s_chunk, g0), gs_pad.reshape(bs * s_chunk, n_gate_lanes),
            gs_pad[:, :, :2 * h_m].transpose(0, 2, 1), state_mlstm_c[l], state_mlstm_n[l], state_mlstm_m[l],
            h_norm_g[l], batch=bs, seq=s_chunk, chunk=s_chunk)
        hm_s = hm_s.reshape(bs, s_chunk, d_m)[:, :t_new].reshape(n_s, d_m)

        q_pad = _pad_rows(zs[:, :, g0:g0 + d_a], TPAD)
        kn_pad = _pad_rows(zs[:, :, g0 + d_a:g0 + 2 * d_a], TPAD)
        vn_pad = _pad_rows(zs[:, :, g0 + 2 * d_a:g0 + 3 * d_a], TPAD)
        k_mean, ms, ls, op = moba_pages(page_table, q_pad, cache_k, cache_v, l, q_norm_g[l], n_heads=h_a)
        ha_s, k_s = moba_combine(q_pad, kn_pad, vn_pad, k_mean, ms, ls, op, q_norm_g[l], k_norm_g[l],
                                 n_heads=h_a, t_new=t_new)
        ha_s = ha_s[:, :t_new].reshape(n_s, d_a)
        k_s = k_s[:, :t_new].reshape(bs, t_new, h_a, HEAD_DIM)
        v_s = vn_pad[:, :t_new].reshape(bs, t_new, h_a, HEAD_DIM)

        zc = _pad_rows(zs[:, :, g0 + 3 * d_a:], TPAD).reshape(bs * TPAD, 3 * d_c)
        hc_s, cv_s = short_conv(zc, 0, state_conv[l], conv_w[l], batch=bs, seq=TPAD, rows=TPAD, valid=t_new)
        hc_s = hc_s.reshape(bs, TPAD, d_c)[:, :t_new].reshape(n_s, d_c)

        mix = jnp.concatenate([jnp.concatenate([hm_p, ha_p, hc_p], axis=1),
                               jnp.concatenate([hm_s, ha_s, hc_s], axis=1)], axis=0)
        h = mm_resid(mix, bf(w_out), h, alpha=1.0, tm=1024, tn=512)

        g = gate_up(rmsnorm_bf16(h, g_ffn2[l]), bf(w1_b), bf(w3_b))
        h = mm_resid(g, bf(w2_b), h, alpha=0.5, tm=256, tn=512)

        p_l = jnp.concatenate([p_prompt[l].reshape(n_p, -1), p_sample[l].reshape(n_s, -1)], axis=0)
        h = ple(rmsnorm_bf16(h, g_ple[l]), bf(w_pgate), p_l, bf(w_ple), h)

        outs["kp"].append(k_p.reshape(bp, seq, h_a, HEAD_DIM))
        outs["vp"].append(v_p.reshape(bp, seq, h_a, HEAD_DIM))
        outs["ks"].append(k_s)
        outs["vs"].append(v_s)
        outs["cp"].append(c_p); outs["np"].append(nn_p); outs["mp"].append(m_p)
        outs["cs"].append(c_s); outs["ns"].append(nn_s); outs["ms"].append(m_s)
        outs["cvp"].append(cv_p); outs["cvs"].append(cv_s)

    st = lambda k: jnp.stack(outs[k])
    return (h[:n_p].reshape(bp, seq, d_model), h[n_p:].reshape(bs, t_new, d_model),
            st("kp"), st("vp"), st("ks"), st("vs"),
            st("cp"), st("np"), st("mp"), st("cs"), st("ns"), st("ms"), st("cvp"), st("cvs"))
```

```python
import functools
import math

import jax
import jax.numpy as jnp
from jax import lax
from jax.experimental import pallas as pl
from jax.experimental.pallas import tpu as pltpu

F32 = jnp.float32
BF16 = jnp.bfloat16

HEAD_DIM = 128
PAGE_SIZE = 128
MOBA_BLOCK = 256
MOBA_TOPK = 3
RMS_EPS = 1e-6
NEG = -1e30
MLSTM_CHUNK = 256
V7X_VMEM_LIMIT = 56 * 1024 * 1024


def _cparams(sem, vmem=V7X_VMEM_LIMIT):
    return pltpu.CompilerParams(dimension_semantics=sem, vmem_limit_bytes=vmem)


def _split3(x):
    hi = x.astype(BF16)
    r1 = x - hi.astype(F32)
    mid = r1.astype(BF16)
    lo = (r1 - mid.astype(F32)).astype(BF16)
    return hi, mid, lo


def _dot_nn(a, b):
    return jnp.dot(a, b, preferred_element_type=F32)


def _dot_nt(a, b):
    return lax.dot_general(a, b, (((1,), (1,)), ((), ())), preferred_element_type=F32)


def _dot_tn(a, b):
    return lax.dot_general(a, b, (((0,), (0,)), ((), ())), preferred_element_type=F32)


def _dot_hp(a, b, dot):
    a3, b3 = _split3(a), _split3(b)
    out = None
    for i, j in ((2, 0), (0, 2), (1, 1), (1, 0), (0, 1), (0, 0)):
        t = dot(a3[i], b3[j])
        out = t if out is None else out + t
    return out


def _dot_3x(a, b, dot):
    a_hi, a_mid, _ = _split3(a)
    b_hi, b_mid, _ = _split3(b)
    return dot(a_mid, b_hi) + dot(a_hi, b_mid) + dot(a_hi, b_hi)


def _dot_hp_exact_lhs(a_bf16, b, dot):
    b3 = _split3(b)
    return dot(a_bf16, b3[2]) + dot(a_bf16, b3[1]) + dot(a_bf16, b3[0])


def _dot_hp_exact_rhs(a, b_bf16, dot):
    a3 = _split3(a)
    return dot(a3[2], b_bf16) + dot(a3[1], b_bf16) + dot(a3[0], b_bf16)


def _rms_rows(x, g):
    return x * lax.rsqrt(jnp.mean(x * x, axis=-1, keepdims=True) + RMS_EPS) * g


def _log_sigmoid(x):
    return jnp.minimum(x, 0.0) - jnp.log1p(jnp.exp(-jnp.abs(x)))


def _sigmoid(x):
    return 1.0 / (1.0 + jnp.exp(-x))


def _rmsnorm_kernel(x_ref, g_ref, o_ref):
    o_ref[...] = _rms_rows(x_ref[...], g_ref[...]).astype(o_ref.dtype)


def _rmsnorm_gates_kernel(x_ref, g_ref, wg_ref, bg_ref, o_ref, gates_ref):
    y = _rms_rows(x_ref[...], g_ref[...])
    o_ref[...] = y.astype(o_ref.dtype)
    gates_ref[...] = _dot_3x(y, wg_ref[...], _dot_nn) + bg_ref[...]


def rmsnorm_bf16(x, g, *, tm=512):
    m, d = x.shape
    return pl.pallas_call(
        _rmsnorm_kernel,
        grid=(pl.cdiv(m, tm),),
        in_specs=[pl.BlockSpec((tm, d), lambda i: (i, 0)),
                  pl.BlockSpec((1, d), lambda i: (0, 0))],
        out_specs=pl.BlockSpec((tm, d), lambda i: (i, 0)),
        out_shape=jax.ShapeDtypeStruct((m, d), BF16),
        compiler_params=_cparams(("parallel",)),
        name="rmsnorm",
    )(x, g.reshape(1, d))


def rmsnorm_gates(x, g, w, layer, col0, bg, *, tm=512):
    m, d = x.shape
    ng = bg.shape[1]
    assert col0 % ng == 0
    return pl.pallas_call(
        _rmsnorm_gates_kernel,
        grid=(pl.cdiv(m, tm),),
        in_specs=[pl.BlockSpec((tm, d), lambda i: (i, 0)),
                  pl.BlockSpec((1, d), lambda i: (0, 0)),
                  pl.BlockSpec((None, d, ng), lambda i: (layer, 0, col0 // ng)),
                  pl.BlockSpec((1, ng), lambda i: (0, 0))],
        out_specs=[pl.BlockSpec((tm, d), lambda i: (i, 0)),
                   pl.BlockSpec((tm, ng), lambda i: (i, 0))],
        out_shape=[jax.ShapeDtypeStruct((m, d), BF16),
                   jax.ShapeDtypeStruct((m, ng), F32)],
        compiler_params=_cparams(("parallel",)),
        name="rmsnorm_gates",
    )(x, g.reshape(1, d), w, bg)


def _row_cases(m, tm, body):
    nm = pl.cdiv(m, tm)
    rem = m - (nm - 1) * tm
    if rem == tm:
        body(tm)
        return
    i = pl.program_id(1)

    @pl.when(i < nm - 1)
    def _():
        body(tm)

    @pl.when(i == nm - 1)
    def _():
        body(rem)


def _cast_weights_once(pairs):
    @pl.when(pl.program_id(1) == 0)
    def _():
        for w_ref, w_sc in pairs:
            w_sc[...] = w_ref[...].astype(BF16)


def _gate_up_kernel(a_ref, w1_ref, w3_ref, o_ref, w1_sc, w3_sc, *, m, tm):
    _cast_weights_once(((w1_ref, w1_sc), (w3_ref, w3_sc)))

    def body(rows):
        a = a_ref[:rows, :]
        x1 = jnp.dot(a, w1_sc[...], preferred_element_type=F32)
        x3 = jnp.dot(a, w3_sc[...], preferred_element_type=F32)
        o_ref[:rows, :] = (x1 * _sigmoid(x1) * x3).astype(o_ref.dtype)
    _row_cases(m, tm, body)


def gate_up(a, w1, w3, layer, *, tm=1024, tn=256):
    m, k = a.shape
    n = w1.shape[2]
    wspec = pl.BlockSpec((None, k, tn), lambda j, i: (layer, 0, j))
    return pl.pallas_call(
        functools.partial(_gate_up_kernel, m=m, tm=tm),
        grid=(pl.cdiv(n, tn), pl.cdiv(m, tm)),
        in_specs=[pl.BlockSpec((tm, k), lambda j, i: (i, 0)), wspec, wspec],
        out_specs=pl.BlockSpec((tm, tn), lambda j, i: (i, j)),
        out_shape=jax.ShapeDtypeStruct((m, n), BF16),
        scratch_shapes=[pltpu.VMEM((k, tn), BF16), pltpu.VMEM((k, tn), BF16)],
        compiler_params=_cparams(("parallel", "arbitrary")),
        name="gate_up",
    )(a, w1, w3)


def _mm_resid_kernel(a_ref, w_ref, h_ref, o_ref, w_sc, *, m, tm, alpha):
    _cast_weights_once(((w_ref, w_sc),))

    def body(rows):
        acc = jnp.dot(a_ref[:rows, :], w_sc[...], preferred_element_type=F32)
        o_ref[:rows, :] = h_ref[:rows, :] + alpha * acc
    _row_cases(m, tm, body)


def mm_resid(a, w, layer, h, *, alpha, tm, tn, k_split=1):
    m, k = a.shape
    n = w.shape[2]
    kc = k // k_split
    assert kc * k_split == k and kc % 128 == 0
    for c in range(k_split):
        h = pl.pallas_call(
            functools.partial(_mm_resid_kernel, m=m, tm=tm, alpha=alpha),
            grid=(pl.cdiv(n, tn), pl.cdiv(m, tm)),
            in_specs=[pl.BlockSpec((tm, kc), lambda j, i, c=c: (i, c)),
                      pl.BlockSpec((None, kc, tn), lambda j, i, c=c: (layer, c, j)),
                      pl.BlockSpec((tm, tn), lambda j, i: (i, j))],
            out_specs=pl.BlockSpec((tm, tn), lambda j, i: (i, j)),
            out_shape=jax.ShapeDtypeStruct((m, n), F32),
            scratch_shapes=[pltpu.VMEM((kc, tn), BF16)],
            compiler_params=_cparams(("parallel", "arbitrary")),
            name="mm_resid",
        )(a, w, h)
    return h


def _in_proj_kernel(a_ref, w_ref, wx_ref, o_ref, w_sc, *, m, tm, n_head_tiles, shift):
    j = pl.program_id(0)
    tn = w_sc.shape[1]

    @pl.when((pl.program_id(1) == 0) & (j < n_head_tiles))
    def _():
        w_sc[...] = w_ref[...].astype(BF16)

    @pl.when((pl.program_id(1) == 0) & (j >= n_head_tiles))
    def _():
        k = w_sc.shape[0]
        rc = math.gcd(k, 512)
        lane = lax.broadcasted_iota(jnp.int32, (rc, 128), 1)
        for r in range(0, k, rc):
            main = pltpu.roll(w_ref[r:r + rc, :], tn - shift, 1)
            nxt = pltpu.roll(wx_ref[r:r + rc, :], 128 - shift, 1)
            tail = jnp.where(lane < 128 - shift, main[:, tn - 128:], nxt)
            if tn > 128:
                w_sc[r:r + rc, :tn - 128] = main[:, :tn - 128].astype(BF16)
            w_sc[r:r + rc, tn - 128:] = tail.astype(BF16)

    def body(rows):
        o_ref[:rows, :] = jnp.dot(a_ref[:rows, :], w_sc[...], preferred_element_type=F32)
    _row_cases(m, tm, body)


def in_proj(a, w, layer, *, n_head, shift, tm=1024):
    m, k = a.shape
    n = w.shape[2] - shift
    tn = 512 if (n % 512 == 0 and n_head % 512 == 0) else 256
    assert n % tn == 0 and n_head % tn == 0 and shift < 128
    return pl.pallas_call(
        functools.partial(_in_proj_kernel, m=m, tm=tm, n_head_tiles=n_head // tn, shift=shift),
        grid=(n // tn, pl.cdiv(m, tm)),
        in_specs=[pl.BlockSpec((tm, k), lambda j, i: (i, 0)),
                  pl.BlockSpec((None, k, tn), lambda j, i: (layer, 0, j)),
                  pl.BlockSpec((None, k, 128), lambda j, i: (layer, 0, (j + 1) * (tn // 128)))],
        out_specs=pl.BlockSpec((tm, tn), lambda j, i: (i, j)),
        out_shape=jax.ShapeDtypeStruct((m, n), F32),
        scratch_shapes=[pltpu.VMEM((k, tn), BF16)],
        compiler_params=_cparams(("parallel", "arbitrary")),
        name="in_proj",
    )(a, w, w)


def _ple_kernel(a_ref, wg_ref, p_ref, wp_ref, h_ref, o_ref, wg_sc, wp_sc, *, m, tm):
    _cast_weights_once(((wg_ref, wg_sc), (wp_ref, wp_sc)))

    def body(rows):
        gate = _sigmoid(jnp.dot(a_ref[:rows, :], wg_sc[...], preferred_element_type=F32))
        pe = jnp.dot(p_ref[:rows, :].astype(BF16), wp_sc[...], preferred_element_type=F32)
        o_ref[:rows, :] = h_ref[:rows, :] + gate * pe
    _row_cases(m, tm, body)


def ple(a, wg, p, wp, layer, h, *, row0=0, rows=None, tm=1024, tn=512):
    k = a.shape[1]
    m = a.shape[0] if rows is None else rows
    n = wg.shape[2]
    kp = p.shape[1]
    assert row0 % tm == 0
    r0 = row0 // tm
    return pl.pallas_call(
        functools.partial(_ple_kernel, m=m, tm=tm),
        grid=(pl.cdiv(n, tn), pl.cdiv(m, tm)),
        in_specs=[pl.BlockSpec((tm, k), lambda j, i: (r0 + i, 0)),
                  pl.BlockSpec((None, k, tn), lambda j, i: (layer, 0, j)),
                  pl.BlockSpec((tm, kp), lambda j, i: (i, 0)),
                  pl.BlockSpec((None, kp, tn), lambda j, i: (layer, 0, j)),
                  pl.BlockSpec((tm, tn), lambda j, i: (r0 + i, j))],
        out_specs=pl.BlockSpec((tm, tn), lambda j, i: (i, j)),
        out_shape=jax.ShapeDtypeStruct((m, n), F32),
        scratch_shapes=[pltpu.VMEM((k, tn), BF16), pltpu.VMEM((kp, tn), BF16)],
        compiler_params=_cparams(("parallel", "arbitrary")),
        name="ple",
    )(a, wg, p, wp, h)


def _mlstm_kernel(zq_ref, zk_ref, zv_ref, zo_ref, gc_ref, gr_ref, c0_ref, n0_ref, m0_ref, hg_ref,
                  hm_ref, c1_ref, n1_ref, m1_ref, ct_sc, n_sc, m_sc, *, n_heads, chunk):
    c = pl.program_id(1)
    last = pl.num_programs(1) - 1
    L = chunk
    scale = HEAD_DIM ** -0.5

    @pl.when(c == 0)
    def _():
        for h in range(n_heads):
            ct_sc[h] = c0_ref[h].T
        n_sc[...] = n0_ref[...]
        m_sc[...] = m0_ref[...]

    row = lax.broadcasted_iota(jnp.int32, (L, L), 0)
    col = lax.broadcasted_iota(jnp.int32, (L, L), 1)
    causal = col <= row
    tri_lo = jnp.where(causal, 1.0, 0.0).astype(BF16)
    tri_up = jnp.where(row <= col, 1.0, 0.0).astype(BF16)

    gc = gc_ref[...]
    gr = gr_ref[...]
    f_cols = _dot_hp_exact_lhs(tri_lo, _log_sigmoid(gc), _dot_nn)
    f_rows = _dot_hp_exact_rhs(_log_sigmoid(gr), tri_up, _dot_nn)

    for h in range(n_heads):
        lanes = slice(h * HEAD_DIM, (h + 1) * HEAD_DIM)
        q = zq_ref[:, lanes]
        k = zk_ref[:, lanes] * scale
        v = zv_ref[:, lanes]
        fc = f_cols[:, n_heads + h:n_heads + h + 1]
        fr = f_rows[n_heads + h:n_heads + h + 1, :]
        ic = gc[:, h:h + 1]
        ir = gr[h:h + 1, :]
        m_prev = m_sc[h:h + 1, 0:1]
        n_prev = n_sc[h:h + 1, :]
        ct_prev = ct_sc[h]

        dmat = jnp.where(causal, fc - fr + ir, NEG)
        inter = fc + m_prev
        m_t = jnp.maximum(inter, jnp.max(dmat, axis=-1, keepdims=True))
        w_intra = jnp.exp(dmat - m_t)
        w_inter = jnp.exp(inter - m_t)
        qb, kb, vb = q.astype(BF16), k.astype(BF16), v.astype(BF16)
        a = w_intra * _dot_nt(qb, kb)
        num = (w_inter * jnp.dot(qb, ct_prev.astype(BF16), preferred_element_type=F32)
               + jnp.dot(a.astype(BF16), vb, preferred_element_type=F32))
        den = w_inter * jnp.sum(q * n_prev, axis=-1, keepdims=True) + jnp.sum(a, axis=-1, keepdims=True)
        hval = num / jnp.maximum(jnp.abs(den), jnp.exp(-m_t))

        m_end = m_t[L - 1:L, :]
        w_s = jnp.exp(fc[L - 1:L, :] - fc + ic - m_end)
        decay = jnp.exp(inter[L - 1:L, :] - m_end)
        kw = w_s * k
        ct_new = decay * ct_prev + _dot_tn(kw.astype(BF16), vb)
        n_new = decay * n_prev + jnp.sum(kw, axis=0, keepdims=True)
        ct_sc[h] = ct_new
        n_sc[h:h + 1, :] = n_new
        m_sc[h:h + 1, :] = jnp.broadcast_to(m_end, (1, HEAD_DIM))

        hn = _rms_rows(hval, hg_ref[:, lanes])
        hm_ref[:, lanes] = (hn * _sigmoid(zo_ref[:, lanes])).astype(hm_ref.dtype)

    @pl.when(c == last)
    def _():
        for h in range(n_heads):
            c1_ref[h] = ct_sc[h].T
        n1_ref[...] = n_sc[...]
        m1_ref[...] = m_sc[...]


def mlstm(zz, gates_c, gates_r, c0, n0, m0, h_norm_g, *, batch, seq, chunk):
    n_heads = c0.shape[1]
    dm = n_heads * HEAD_DIM
    nc = seq // chunk
    m0b = jnp.broadcast_to(m0[:, :, None], (batch, n_heads, HEAD_DIM))
    zspec = lambda g: pl.BlockSpec((chunk, dm), lambda b, c, g=g: (b * nc + c, g))
    state = lambda *shape: pl.BlockSpec((None,) + shape, lambda b, c: (b,) + (0,) * len(shape))
    hm, c1, n1, m1 = pl.pallas_call(
        functools.partial(_mlstm_kernel, n_heads=n_heads, chunk=chunk),
        grid=(batch, nc),
        in_specs=[zspec(0), zspec(1), zspec(2), zspec(3),
                  pl.BlockSpec((chunk, gates_c.shape[1]), lambda b, c: (b * nc + c, 0)),
                  pl.BlockSpec((None, 2 * n_heads, chunk), lambda b, c: (b, 0, c)),
                  state(n_heads, HEAD_DIM, HEAD_DIM), state(n_heads, HEAD_DIM), state(n_heads, HEAD_DIM),
                  pl.BlockSpec((1, dm), lambda b, c: (0, 0))],
        out_specs=[pl.BlockSpec((chunk, dm), lambda b, c: (b * nc + c, 0)),
                   state(n_heads, HEAD_DIM, HEAD_DIM), state(n_heads, HEAD_DIM), state(n_heads, HEAD_DIM)],
        out_shape=[jax.ShapeDtypeStruct((batch * seq, dm), BF16),
                   jax.ShapeDtypeStruct((batch, n_heads, HEAD_DIM, HEAD_DIM), F32),
                   jax.ShapeDtypeStruct((batch, n_heads, HEAD_DIM), F32),
                   jax.ShapeDtypeStruct((batch, n_heads, HEAD_DIM), F32)],
        scratch_shapes=[pltpu.VMEM((n_heads, HEAD_DIM, HEAD_DIM), F32),
                        pltpu.VMEM((n_heads, HEAD_DIM), F32),
                        pltpu.VMEM((n_heads, HEAD_DIM), F32)],
        compiler_params=_cparams(("parallel", "arbitrary")),
        name="mlstm",
    )(zz, zz, zz, zz, gates_c, gates_r, c0, n0, m0b, h_norm_g.reshape(1, dm))
    return hm, c1, n1, m1[:, :, 0]


def _conv_kernel(cb_ref, cc_ref, cx_ref, s0_ref, w_ref, hc_ref, s1_ref, carry_sc, *, rows, last_row):
    t = pl.program_id(1)
    last = pl.num_programs(1) - 1

    @pl.when(t == 0)
    def _():
        carry_sc[...] = s0_ref[...]

    u = cc_ref[...] * cx_ref[...]
    prev2 = carry_sc[0:1, :]
    prev1 = carry_sc[1:2, :]
    r = lax.broadcasted_iota(jnp.int32, u.shape, 0)
    u1 = jnp.where(r == 0, prev1, pltpu.roll(u, 1, 0))
    u2 = jnp.where(r == 0, prev2, jnp.where(r == 1, prev1, pltpu.roll(u, 2, 0)))
    y = w_ref[0:1, :] * u2 + w_ref[1:2, :] * u1 + w_ref[2:3, :] * u
    hc_ref[...] = (cb_ref[...] * y).astype(hc_ref.dtype)
    carry_sc[...] = u[rows - 2:rows, :]

    @pl.when(t == last)
    def _():
        s1_ref[...] = u[last_row - 1:last_row + 1, :]


def short_conv(zz, col0, s0, w, *, batch, seq, rows, valid):
    c = s0.shape[-1]
    nt = seq // rows
    last_row = valid - 1 - (nt - 1) * rows
    zspec = lambda g: pl.BlockSpec((rows, c), lambda b, t, g=g: (b * nt + t, col0 + g))
    return pl.pallas_call(
        functools.partial(_conv_kernel, rows=rows, last_row=last_row),
        grid=(batch, nt),
        in_specs=[zspec(0), zspec(1), zspec(2),
                  pl.BlockSpec((None, 2, c), lambda b, t: (b, 0, 0)),
                  pl.BlockSpec((3, c), lambda b, t: (0, 0))],
        out_specs=[pl.BlockSpec((rows, c), lambda b, t: (b * nt + t, 0)),
                   pl.BlockSpec((None, 2, c), lambda b, t: (b, 0, 0))],
        out_shape=[jax.ShapeDtypeStruct((batch * seq, c), BF16),
                   jax.ShapeDtypeStruct((batch, 2, c), F32)],
        scratch_shapes=[pltpu.VMEM((2, c), F32)],
        compiler_params=_cparams(("parallel", "arbitrary")),
        name="short_conv",
    )(zz, zz, zz, s0, w)


def _top_mask_rows(gate, valid, n_keep):
    n = gate.shape[0]
    jidx = lax.broadcasted_iota(jnp.int32, gate.shape, 0)
    g = jnp.where(valid, gate, NEG)
    rank = jnp.zeros(gate.shape, F32)
    for j in range(n):
        gj = g[j:j + 1, :]
        beats = (gj > g) | ((gj == g) & (jidx > j))
        rank = rank + jnp.where(beats, 1.0, 0.0)
    return jnp.where(valid & (rank < n_keep), 1.0, 0.0)


def _moba_prompt_kernel(q_ref, k_ref, v_ref, gq_ref, gk_ref, ko_ref, vo_ref, ha_ref, *, seq):
    nb = seq // MOBA_BLOCK
    scale = HEAD_DIM ** -0.5
    qn = _rms_rows(q_ref[...], gq_ref[...])
    kn = _rms_rows(k_ref[...], gk_ref[...])
    v = v_ref[...]
    ko_ref[...] = kn
    vo_ref[...] = v
    qb, kb, vb = qn.astype(BF16), kn.astype(BF16), v.astype(BF16)

    avg = jnp.where(lax.broadcasted_iota(jnp.int32, (HEAD_DIM, seq), 1) // MOBA_BLOCK
                    == lax.broadcasted_iota(jnp.int32, (HEAD_DIM, seq), 0), 1.0 / MOBA_BLOCK, 0.0).astype(BF16)
    k_mean = _dot_hp_exact_lhs(avg, kn, _dot_nn)
    gate_t = _dot_hp(k_mean, qn, _dot_nt)
    own = lax.broadcasted_iota(jnp.int32, gate_t.shape, 1) // MOBA_BLOCK
    blk = lax.broadcasted_iota(jnp.int32, gate_t.shape, 0)
    sel = _top_mask_rows(gate_t[:8 * pl.cdiv(nb, 8)], (blk < own)[:8 * pl.cdiv(nb, 8)], MOBA_TOPK)
    sel = jnp.concatenate([sel, jnp.zeros((HEAD_DIM - sel.shape[0], seq), F32)], axis=0).T

    r = lax.broadcasted_iota(jnp.int32, (MOBA_BLOCK, MOBA_BLOCK), 0)
    c = lax.broadcasted_iota(jnp.int32, (MOBA_BLOCK, MOBA_BLOCK), 1)
    for i in range(nb):
        rows = slice(i * MOBA_BLOCK, (i + 1) * MOBA_BLOCK)
        kv = (i + 1) * MOBA_BLOCK
        s = _dot_nt(qb[rows], kb[:kv]) * scale
        parts = [jnp.where(sel[rows, j:j + 1] > 0.5, s[:, j * MOBA_BLOCK:(j + 1) * MOBA_BLOCK], NEG)
                 for j in range(i)]
        parts.append(jnp.where(c <= r, s[:, i * MOBA_BLOCK:], NEG))
        s = parts[0] if i == 0 else jnp.concatenate(parts, axis=1)
        p = jnp.exp(s - jnp.max(s, axis=-1, keepdims=True))
        o = jnp.dot(p.astype(BF16), vb[:kv], preferred_element_type=F32) / jnp.sum(p, axis=-1, keepdims=True)
        ha_ref[rows, :] = o.astype(ha_ref.dtype)


def moba_prompt(zz, colq, gq, gk, *, batch, seq, n_heads):
    d = HEAD_DIM
    zspec = lambda g: pl.BlockSpec((seq, d), lambda b, h, g=g: (b, colq + g * n_heads + h))
    ospec = pl.BlockSpec((None, seq, d), lambda b, h: (b, 0, h))
    ko, vo, ha = pl.pallas_call(
        functools.partial(_moba_prompt_kernel, seq=seq),
        grid=(batch, n_heads),
        in_specs=[zspec(0), zspec(1), zspec(2),
                  pl.BlockSpec((1, d), lambda b, h: (0, 0)),
                  pl.BlockSpec((1, d), lambda b, h: (0, 0))],
        out_specs=[ospec, ospec, pl.BlockSpec((seq, d), lambda b, h: (b, h))],
        out_shape=[jax.ShapeDtypeStruct((batch, seq, n_heads * d), F32),
                   jax.ShapeDtypeStruct((batch, seq, n_heads * d), F32),
                   jax.ShapeDtypeStruct((batch * seq, n_heads * d), BF16)],
        compiler_params=_cparams(("parallel", "parallel")),
        name="moba_prompt",
    )(zz, zz, zz, gq.reshape(1, d), gk.reshape(1, d))
    return ha, ko, vo


def _moba_pages_kernel(pt_ref, q_ref, k0_ref, k1_ref, v0_ref, v1_ref, gq_ref,
                       km_ref, ms_ref, ls_ref, op_ref, q_sc, *, n_heads):
    del pt_ref
    j = pl.program_id(1)
    scale = HEAD_DIM ** -0.5
    rows = q_sc.shape[0]

    @pl.when(j == 0)
    def _():
        q_sc[...] = _rms_rows(q_ref[...], gq_ref[...]).astype(BF16)
        ms_ref[...] = jnp.zeros(ms_ref.shape, F32)
        ls_ref[...] = jnp.zeros(ls_ref.shape, F32)

    k0, k1 = k0_ref[...], k1_ref[...]
    km_ref[...] = (jnp.sum(k0, axis=0) + jnp.sum(k1, axis=0)) * (1.0 / MOBA_BLOCK)
    flat = lambda x: x.reshape(PAGE_SIZE * n_heads, HEAD_DIM).astype(BF16)
    kb = jnp.concatenate([flat(k0), flat(k1)], axis=0)
    vb = jnp.concatenate([flat(v0_ref[...]), flat(v1_ref[...])], axis=0)
    s = _dot_nt(q_sc[...], kb) * scale
    qhead = lax.broadcasted_iota(jnp.int32, s.shape, 0) % n_heads
    khead = lax.broadcasted_iota(jnp.int32, s.shape, 1) % n_heads
    s = jnp.where(qhead == khead, s, NEG)
    m = jnp.max(s, axis=-1, keepdims=True)
    p = jnp.exp(s - m)
    l = jnp.sum(p, axis=-1, keepdims=True)
    op_ref[...] = jnp.dot(p.astype(BF16), vb, preferred_element_type=F32)
    lane = lax.broadcasted_iota(jnp.int32, (rows, ms_ref.shape[-1]), 1)
    ms_ref[...] = jnp.where(lane == j, m, ms_ref[...])
    ls_ref[...] = jnp.where(lane == j, l, ls_ref[...])


def moba_pages(page_table, q, cache_k, cache_v, layer, gq, *, n_heads):
    batch, n_pages = page_table.shape
    nblk = n_pages * PAGE_SIZE // MOBA_BLOCK
    assert MOBA_BLOCK == 2 * PAGE_SIZE
    d = HEAD_DIM
    rows = q.shape[1]
    stat_lanes = 128 * pl.cdiv(nblk, 128)
    page = lambda o: pl.BlockSpec((None, None, PAGE_SIZE, n_heads, d),
                                  lambda b, j, pt, o=o: (layer, pt[b * n_pages + 2 * j + o], 0, 0, 0))
    grid_spec = pltpu.PrefetchScalarGridSpec(
        num_scalar_prefetch=1,
        grid=(batch, nblk),
        in_specs=[pl.BlockSpec((None, rows, d), lambda b, j, pt: (b, 0, 0)),
                  page(0), page(1), page(0), page(1),
                  pl.BlockSpec((1, d), lambda b, j, pt: (0, 0))],
        out_specs=[pl.BlockSpec((None, None, n_heads, d), lambda b, j, pt: (b, j, 0, 0)),
                   pl.BlockSpec((None, rows, stat_lanes), lambda b, j, pt: (b, 0, 0)),
                   pl.BlockSpec((None, rows, stat_lanes), lambda b, j, pt: (b, 0, 0)),
                   pl.BlockSpec((None, None, rows, d), lambda b, j, pt: (b, j, 0, 0))],
        scratch_shapes=[pltpu.VMEM((rows, d), BF16)],
    )
    return pl.pallas_call(
        functools.partial(_moba_pages_kernel, n_heads=n_heads),
        grid_spec=grid_spec,
        out_shape=[jax.ShapeDtypeStruct((batch, nblk, n_heads, d), F32),
                   jax.ShapeDtypeStruct((batch, rows, stat_lanes), F32),
                   jax.ShapeDtypeStruct((batch, rows, stat_lanes), F32),
                   jax.ShapeDtypeStruct((batch, nblk, rows, d), F32)],
        compiler_params=_cparams(("parallel", "arbitrary")),
        name="moba_pages",
    )(page_table.reshape(-1), q, cache_k, cache_k, cache_v, cache_v, gq.reshape(1, d))


def _moba_combine_kernel(q_ref, kn_ref, vn_ref, km_ref, ms_ref, ls_ref, op_ref, gq_ref, gk_ref,
                         ha_ref, ko_ref, *, n_heads, nblk):
    scale = HEAD_DIM ** -0.5
    q = _rms_rows(q_ref[...], gq_ref[...])
    kn = _rms_rows(kn_ref[...], gk_ref[...])
    ko_ref[...] = kn

    km = km_ref[...].reshape(nblk * n_heads, HEAD_DIM)
    g_all = _dot_hp(q, km, _dot_nt)
    rhead = lax.broadcasted_iota(jnp.int32, g_all.shape, 0) % n_heads
    chead = lax.broadcasted_iota(jnp.int32, g_all.shape, 1) % n_heads
    g_all = jnp.where(rhead == chead, g_all, 0.0)
    pick = jnp.where(lax.broadcasted_iota(jnp.int32, (nblk * n_heads, nblk), 0) // n_heads
                     == lax.broadcasted_iota(jnp.int32, (nblk * n_heads, nblk), 1), 1.0, 0.0).astype(BF16)
    gate = _dot_hp_exact_rhs(g_all, pick, _dot_nn)

    lane = lax.broadcasted_iota(jnp.int32, gate.shape, 1)
    rank = jnp.zeros(gate.shape, F32)
    for j in range(nblk):
        gj = gate[:, j:j + 1]
        beats = (gj > gate) | ((gj == gate) & (lane > j))
        rank = rank + jnp.where(beats, 1.0, 0.0)
    sel = rank < min(MOBA_TOPK, nblk)

    s_own = _dot_nt(q.astype(BF16), kn.astype(BF16)) * scale
    r = lax.broadcasted_iota(jnp.int32, s_own.shape, 0)
    c = lax.broadcasted_iota(jnp.int32, s_own.shape, 1)
    s_own = jnp.where((r % n_heads == c % n_heads) & (c // n_heads <= r // n_heads), s_own, NEG)

    ms = ms_ref[:, :nblk]
    ls = ls_ref[:, :nblk]
    m_star = jnp.maximum(jnp.max(jnp.where(sel, ms, NEG), axis=-1, keepdims=True),
                         jnp.max(s_own, axis=-1, keepdims=True))
    w = jnp.where(sel, jnp.exp(ms - m_star), 0.0)
    p_own = jnp.exp(s_own - m_star)
    l = jnp.sum(w * ls, axis=-1, keepdims=True) + jnp.sum(p_own, axis=-1, keepdims=True)
    o = jnp.dot(p_own.astype(BF16), vn_ref[...].astype(BF16), preferred_element_type=F32)
    for j in range(nblk):
        o = o + w[:, j:j + 1] * op_ref[j]
    ha_ref[...] = (o / l).astype(ha_ref.dtype)


def moba_combine(q, kn, vn, k_mean, ms, ls, op, gq, gk, *, n_heads):
    batch, nblk = k_mean.shape[:2]
    d = HEAD_DIM
    rows = q.shape[1]
    full = lambda a: pl.BlockSpec((None,) + a.shape[1:], lambda b: (b,) + (0,) * (a.ndim - 1))
    vec = pl.BlockSpec((1, d), lambda b: (0, 0))
    return pl.pallas_call(
        functools.partial(_moba_combine_kernel, n_heads=n_heads, nblk=nblk),
        grid=(batch,),
        in_specs=[full(q), full(kn), full(vn), full(k_mean), full(ms), full(ls), full(op), vec, vec],
        out_specs=[pl.BlockSpec((None, rows, d), lambda b: (b, 0, 0)),
                   pl.BlockSpec((None, rows, d), lambda b: (b, 0, 0))],
        out_shape=[jax.ShapeDtypeStruct((batch, rows, d), BF16),
                   jax.ShapeDtypeStruct((batch, rows, d), F32)],
        compiler_params=_cparams(("parallel",)),
        name="moba_combine",
    )(q, kn, vn, k_mean, ms, ls, op, gq.reshape(1, d), gk.reshape(1, d))


CONV_PAD = 8


def _pad_rows(x, rows):
    return jnp.pad(x, ((0, 0), (0, rows - x.shape[1]), (0, 0)))


def kernel(x_prompt, x_sample, cache_k, cache_v, state_mlstm_c, state_mlstm_n, state_mlstm_m, state_conv,
           page_table, p_prompt, p_sample, g_ffn1, w1_a, w3_a, w2_a, g_mix, w_in, b_if, q_norm_g, k_norm_g,
           h_norm_g, conv_w, w_out, g_ffn2, w1_b, w3_b, w2_b, g_ple, w_pgate, w_ple):
    bp, seq, d_model = x_prompt.shape
    bs, t_new, _ = x_sample.shape
    depth = g_ffn1.shape[0]
    h_m = state_mlstm_c.shape[2]
    h_a = cache_k.shape[3]
    d_m, d_a = h_m * HEAD_DIM, h_a * HEAD_DIM
    d_c = d_model - d_m - d_a
    n_p, n_s = bp * seq, bs * t_new
    assert 2 <= t_new <= CONV_PAD and (t_new * h_a) % 8 == 0
    assert seq % MLSTM_CHUNK == 0 and seq % MOBA_BLOCK == 0

    g0 = 4 * d_m
    g1 = g0 + 2 * h_m
    n_gate_lanes = 128
    col_aq = g0 // HEAD_DIM
    col_cb = (g0 + 3 * d_a) // d_c

    h = jnp.concatenate([x_prompt.reshape(n_p, d_model), x_sample.reshape(n_s, d_model)], axis=0)
    outs = {k: [] for k in ("kp", "vp", "ks", "vs", "cp", "np", "mp", "cs", "ns", "ms", "cvp", "cvs")}
    s_chunk = 128

    ffn_split = 2 if (w2_a.shape[1] // 2) % 128 == 0 else 1
    for l in range(depth):
        g = gate_up(rmsnorm_bf16(h, g_ffn1[l]), w1_a, w3_a, l)
        h = mm_resid(g, w2_a, l, h, alpha=0.5, tm=512, tn=512, k_split=ffn_split)

        b_gate = jnp.pad(b_if[l], (0, n_gate_lanes - 2 * h_m)).reshape(1, n_gate_lanes)
        u, gates = rmsnorm_gates(h, g_mix[l], w_in, l, g0, b_gate)
        z = in_proj(u, w_in, l, n_head=g0, shift=2 * h_m)

        gr_p = gates[:n_p, :2 * h_m].reshape(bp, seq, 2 * h_m).transpose(0, 2, 1)
        hm_p, c_p, nn_p, m_p = mlstm(
            z, gates, gr_p, jnp.zeros((bp, h_m, HEAD_DIM, HEAD_DIM), F32), jnp.zeros((bp, h_m, HEAD_DIM), F32),
            jnp.zeros((bp, h_m), F32), h_norm_g[l], batch=bp, seq=seq, chunk=MLSTM_CHUNK)
        ha_p, k_p, v_p = moba_prompt(z, col_aq, q_norm_g[l], k_norm_g[l], batch=bp, seq=seq, n_heads=h_a)
        hc_p, cv_p = short_conv(z, col_cb, jnp.zeros((bp, 2, d_c), F32), conv_w[l],
                                batch=bp, seq=seq, rows=512, valid=seq)

        zs = z[n_p:].reshape(bs, t_new, -1)
        gs = gates[n_p:].reshape(bs, t_new, n_gate_lanes)
        pad_gate = jnp.concatenate([jnp.full((h_m,), NEG, F32), jnp.full((h_m,), -NEG, F32),
                                    jnp.zeros((n_gate_lanes - 2 * h_m,), F32)])
        gs_pad = jnp.concatenate([gs, jnp.broadcast_to(pad_gate, (bs, s_chunk - t_new, n_gate_lanes))], axis=1)
        hm_s, c_s, nn_s, m_s = mlstm(
            _pad_rows(zs[:, :, :g0], s_chunk).reshape(bs * s_chunk, g0), gs_pad.reshape(bs * s_chunk, n_gate_lanes),
            gs_pad[:, :, :2 * h_m].transpose(0, 2, 1), state_mlstm_c[l], state_mlstm_n[l], state_mlstm_m[l],
            h_norm_g[l], batch=bs, seq=s_chunk, chunk=s_chunk)
        hm_s = hm_s.reshape(bs, s_chunk, d_m)[:, :t_new].reshape(n_s, d_m)

        heads = lambda a: a.reshape(bs, t_new * h_a, HEAD_DIM)
        q_s = heads(zs[:, :, g0:g0 + d_a])
        kn_s = heads(zs[:, :, g0 + d_a:g0 + 2 * d_a])
        vn_s = heads(zs[:, :, g0 + 2 * d_a:g0 + 3 * d_a])
        k_mean, ms, ls, op = moba_pages(page_table, q_s, cache_k, cache_v, l, q_norm_g[l], n_heads=h_a)
        ha_s, k_s = moba_combine(q_s, kn_s, vn_s, k_mean, ms, ls, op, q_norm_g[l], k_norm_g[l], n_heads=h_a)
        ha_s = ha_s.reshape(n_s, d_a)
        k_s = k_s.reshape(bs, t_new, h_a, HEAD_DIM)
        v_s = vn_s.reshape(bs, t_new, h_a, HEAD_DIM)

        zc = _pad_rows(zs[:, :, g0 + 3 * d_a:], CONV_PAD).reshape(bs * CONV_PAD, 3 * d_c)
        hc_s, cv_s = short_conv(zc, 0, state_conv[l], conv_w[l], batch=bs, seq=CONV_PAD, rows=CONV_PAD,
                                valid=t_new)
        hc_s = hc_s.reshape(bs, CONV_PAD, d_c)[:, :t_new].reshape(n_s, d_c)

        mix = jnp.concatenate([jnp.concatenate([hm_p, ha_p, hc_p], axis=1),
                               jnp.concatenate([hm_s, ha_s, hc_s], axis=1)], axis=0)
        h = mm_resid(mix, w_out, l, h, alpha=1.0, tm=1024, tn=512)

        g = gate_up(rmsnorm_bf16(h, g_ffn2[l]), w1_b, w3_b, l)
        h = mm_resid(g, w2_b, l, h, alpha=0.5, tm=512, tn=512, k_split=ffn_split)

        n4 = rmsnorm_bf16(h, g_ple[l])
        pp, ps = p_prompt[l].reshape(n_p, -1), p_sample[l].reshape(n_s, -1)
        if l < depth - 1:
            h = ple(n4, w_pgate, jnp.concatenate([pp, ps], axis=0), w_ple, l, h)
        else:
            y_p = ple(n4, w_pgate, pp, w_ple, l, h, row0=0, rows=n_p)
            y_s = ple(n4, w_pgate, ps, w_ple, l, h, row0=n_p, rows=n_s, tm=n_s)

        outs["kp"].append(k_p.reshape(bp, seq, h_a, HEAD_DIM))
        outs["vp"].append(v_p.reshape(bp, seq, h_a, HEAD_DIM))
        outs["ks"].append(k_s)
        outs["vs"].append(v_s)
        outs["cp"].append(c_p); outs["np"].append(nn_p); outs["mp"].append(m_p)
        outs["cs"].append(c_s); outs["ns"].append(nn_s); outs["ms"].append(m_s)
        outs["cvp"].append(cv_p); outs["cvs"].append(cv_s)

    st = lambda k: jnp.stack(outs[k])
    return (y_p.reshape(bp, seq, d_model), y_s.reshape(bs, t_new, d_model),
            st("kp"), st("vp"), st("ks"), st("vs"),
            st("cp"), st("np"), st("mp"), st("cs"), st("ns"), st("ms"), st("cvp"), st("cvs"))
```

```python
import functools
import math

import jax
import jax.numpy as jnp
from jax import lax
from jax.experimental import pallas as pl
from jax.experimental.pallas import tpu as pltpu

F32 = jnp.float32
BF16 = jnp.bfloat16

HEAD_DIM = 128
PAGE_SIZE = 128
MOBA_BLOCK = 256
MOBA_TOPK = 3
RMS_EPS = 1e-6
NEG = -1e30
MLSTM_CHUNK = 256
V7X_VMEM_LIMIT = 60000 * 1024


def _cparams(sem, vmem=V7X_VMEM_LIMIT):
    return pltpu.CompilerParams(dimension_semantics=sem, vmem_limit_bytes=vmem)


def _split3(x):
    hi = x.astype(BF16)
    r1 = x - hi.astype(F32)
    mid = r1.astype(BF16)
    lo = (r1 - mid.astype(F32)).astype(BF16)
    return hi, mid, lo


def _dot_nn(a, b):
    return jnp.dot(a, b, preferred_element_type=F32)


def _dot_nt(a, b):
    return lax.dot_general(a, b, (((1,), (1,)), ((), ())), preferred_element_type=F32)


def _dot_tn(a, b):
    return lax.dot_general(a, b, (((0,), (0,)), ((), ())), preferred_element_type=F32)


def _dot_hp(a, b, dot):
    a3, b3 = _split3(a), _split3(b)
    out = None
    for i, j in ((2, 0), (0, 2), (1, 1), (1, 0), (0, 1), (0, 0)):
        t = dot(a3[i], b3[j])
        out = t if out is None else out + t
    return out


def _dot_3x(a, b, dot):
    a_hi, a_mid, _ = _split3(a)
    b_hi, b_mid, _ = _split3(b)
    return dot(a_mid, b_hi) + dot(a_hi, b_mid) + dot(a_hi, b_hi)


def _dot_hp_exact_lhs(a_bf16, b, dot):
    b3 = _split3(b)
    return dot(a_bf16, b3[2]) + dot(a_bf16, b3[1]) + dot(a_bf16, b3[0])


def _dot_hp_exact_rhs(a, b_bf16, dot):
    a3 = _split3(a)
    return dot(a3[2], b_bf16) + dot(a3[1], b_bf16) + dot(a3[0], b_bf16)


def _rms_rows(x, g):
    return x * lax.rsqrt(jnp.mean(x * x, axis=-1, keepdims=True) + RMS_EPS) * g


def _log_sigmoid(x):
    return jnp.minimum(x, 0.0) - jnp.log1p(jnp.exp(-jnp.abs(x)))


def _sigmoid(x):
    return 1.0 / (1.0 + jnp.exp(-x))


def _rmsnorm_kernel(x_ref, g_ref, o_ref):
    o_ref[...] = _rms_rows(x_ref[...], g_ref[...]).astype(o_ref.dtype)


def _rmsnorm_gates_kernel(x_ref, g_ref, wg_ref, bg_ref, o_ref, gates_ref):
    y = _rms_rows(x_ref[...], g_ref[...])
    o_ref[...] = y.astype(o_ref.dtype)
    gates_ref[...] = _dot_3x(y, wg_ref[...], _dot_nt) + bg_ref[...]


def rmsnorm_bf16(x, g, *, tm=512):
    m, d = x.shape
    return pl.pallas_call(
        _rmsnorm_kernel,
        grid=(pl.cdiv(m, tm),),
        in_specs=[pl.BlockSpec((tm, d), lambda i: (i, 0)),
                  pl.BlockSpec((1, d), lambda i: (0, 0))],
        out_specs=pl.BlockSpec((tm, d), lambda i: (i, 0)),
        out_shape=jax.ShapeDtypeStruct((m, d), BF16),
        compiler_params=_cparams(("parallel",)),
        name="rmsnorm",
    )(x, g.reshape(1, d))


def rmsnorm_gates(x, g, wt, layer, col0, bg, *, tm=512):
    m, d = x.shape
    ng = bg.shape[1]
    assert col0 % ng == 0
    return pl.pallas_call(
        _rmsnorm_gates_kernel,
        grid=(pl.cdiv(m, tm),),
        in_specs=[pl.BlockSpec((tm, d), lambda i: (i, 0)),
                  pl.BlockSpec((1, d), lambda i: (0, 0)),
                  pl.BlockSpec((None, ng, d), lambda i: (layer, col0 // ng, 0)),
                  pl.BlockSpec((1, ng), lambda i: (0, 0))],
        out_specs=[pl.BlockSpec((tm, d), lambda i: (i, 0)),
                   pl.BlockSpec((tm, ng), lambda i: (i, 0))],
        out_shape=[jax.ShapeDtypeStruct((m, d), BF16),
                   jax.ShapeDtypeStruct((m, ng), F32)],
        compiler_params=_cparams(("parallel",)),
        name="rmsnorm_gates",
    )(x, g.reshape(1, d), wt, bg)


BF16_SUBLANES = 16
ROW_TILE = 1040
ROW_TILE_GATE_UP = 1648
ROW_TILE_IN = 1376
ROW_TILE_DOWN = 688


def _row_tile(m, max_tile):
    n = pl.cdiv(m, max_tile)
    return BF16_SUBLANES * pl.cdiv(pl.cdiv(m, n), BF16_SUBLANES)


def _row_cases(m, tm, body):
    nm = pl.cdiv(m, tm)
    rem = m - (nm - 1) * tm
    if rem == tm:
        body(tm)
        return
    i = pl.program_id(1)

    @pl.when(i < nm - 1)
    def _():
        body(tm)

    @pl.when(i == nm - 1)
    def _():
        body(rem)


def _cast_weights_once(pairs):
    @pl.when(pl.program_id(1) == 0)
    def _():
        for w_ref, w_sc in pairs:
            w_sc[...] = w_ref[...].astype(BF16)


def _gate_up_kernel(a_ref, w1_ref, w3_ref, o_ref, w1_sc, w3_sc, *, m, tm):
    _cast_weights_once(((w1_ref, w1_sc), (w3_ref, w3_sc)))

    def body(rows):
        a = a_ref[:rows, :]
        x1 = jnp.dot(a, w1_sc[...], preferred_element_type=F32)
        x3 = jnp.dot(a, w3_sc[...], preferred_element_type=F32)
        o_ref[:rows, :] = (x1 * _sigmoid(x1) * x3).astype(o_ref.dtype)
    _row_cases(m, tm, body)


def gate_up(a, w1, w3, layer, *, tn=256):
    m, k = a.shape
    n = w1.shape[2]
    tm = _row_tile(m, ROW_TILE_GATE_UP)
    wspec = pl.BlockSpec((None, k, tn), lambda j, i: (layer, 0, j))
    return pl.pallas_call(
        functools.partial(_gate_up_kernel, m=m, tm=tm),
        grid=(pl.cdiv(n, tn), pl.cdiv(m, tm)),
        in_specs=[pl.BlockSpec((tm, k), lambda j, i: (i, 0)), wspec, wspec],
        out_specs=pl.BlockSpec((tm, tn), lambda j, i: (i, j)),
        out_shape=jax.ShapeDtypeStruct((m, n), BF16),
        scratch_shapes=[pltpu.VMEM((k, tn), BF16), pltpu.VMEM((k, tn), BF16)],
        compiler_params=_cparams(("parallel", "arbitrary")),
        name="gate_up",
    )(a, w1, w3)


def _mm_resid_kernel(a_ref, w_ref, h_ref, o_ref, w_sc, *, m, tm, alpha):
    _cast_weights_once(((w_ref, w_sc),))

    def body(rows):
        acc = jnp.dot(a_ref[:rows, :], w_sc[...], preferred_element_type=F32)
        o_ref[:rows, :] = h_ref[:rows, :] + alpha * acc
    _row_cases(m, tm, body)


def mm_resid(a, w, layer, h, *, alpha, max_tile, tn=512, k_split=1):
    m, k = a.shape
    n = w.shape[2]
    tm = _row_tile(m, max_tile)
    kc = k // k_split
    assert kc * k_split == k and kc % 128 == 0
    for c in range(k_split):
        h = pl.pallas_call(
            functools.partial(_mm_resid_kernel, m=m, tm=tm, alpha=alpha),
            grid=(pl.cdiv(n, tn), pl.cdiv(m, tm)),
            in_specs=[pl.BlockSpec((tm, kc), lambda j, i, c=c: (i, c)),
                      pl.BlockSpec((None, kc, tn), lambda j, i, c=c: (layer, c, j)),
                      pl.BlockSpec((tm, tn), lambda j, i: (i, j))],
            out_specs=pl.BlockSpec((tm, tn), lambda j, i: (i, j)),
            out_shape=jax.ShapeDtypeStruct((m, n), F32),
            scratch_shapes=[pltpu.VMEM((kc, tn), BF16)],
            compiler_params=_cparams(("parallel", "arbitrary")),
            name="mm_resid",
        )(a, w, h)
    return h


def _in_proj_kernel(a_ref, w_ref, wx_ref, o_ref, w_sc, *, m, tm, n_head_tiles, shift):
    j = pl.program_id(0)
    tn = w_sc.shape[0]

    @pl.when((pl.program_id(1) == 0) & (j < n_head_tiles))
    def _():
        w_sc[...] = w_ref[...].astype(BF16)

    @pl.when((pl.program_id(1) == 0) & (j >= n_head_tiles))
    def _():
        w_sc[:tn - shift, :] = w_ref[shift:, :].astype(BF16)
        w_sc[tn - shift:, :] = wx_ref[...].astype(BF16)

    def body(rows):
        o_ref[:rows, :] = _dot_nt(a_ref[:rows, :], w_sc[...])
    _row_cases(m, tm, body)


def in_proj(a, wt, layer, *, n_head, shift, tn=512):
    m, k = a.shape
    n = wt.shape[1] - shift
    tm = _row_tile(m, ROW_TILE_IN)
    assert n % tn == 0 and n_head % tn == 0 and shift % 8 == 0 and tn % shift == 0
    return pl.pallas_call(
        functools.partial(_in_proj_kernel, m=m, tm=tm, n_head_tiles=n_head // tn, shift=shift),
        grid=(n // tn, pl.cdiv(m, tm)),
        in_specs=[pl.BlockSpec((tm, k), lambda j, i: (i, 0)),
                  pl.BlockSpec((None, tn, k), lambda j, i: (layer, j, 0)),
                  pl.BlockSpec((None, shift, k), lambda j, i: (layer, (j + 1) * (tn // shift), 0))],
        out_specs=pl.BlockSpec((tm, tn), lambda j, i: (i, j)),
        out_shape=jax.ShapeDtypeStruct((m, n), F32),
        scratch_shapes=[pltpu.VMEM((tn, k), BF16)],
        compiler_params=_cparams(("parallel", "arbitrary")),
        name="in_proj",
    )(a, wt, wt)


def _ple_kernel(a_ref, wg_ref, p_ref, wp_ref, h_ref, o_ref, wg_sc, wp_sc, *, m, tm):
    _cast_weights_once(((wg_ref, wg_sc), (wp_ref, wp_sc)))

    def body(rows):
        gate = _sigmoid(jnp.dot(a_ref[:rows, :], wg_sc[...], preferred_element_type=F32))
        pe = jnp.dot(p_ref[:rows, :].astype(BF16), wp_sc[...], preferred_element_type=F32)
        o_ref[:rows, :] = h_ref[:rows, :] + gate * pe
    _row_cases(m, tm, body)


def ple(a, wg, p, wp, layer, h, *, row0=0, rows=None, tn=512):
    k = a.shape[1]
    m = a.shape[0] if rows is None else rows
    tm = _row_tile(m, ROW_TILE)
    n = wg.shape[2]
    kp = p.shape[1]
    assert row0 % tm == 0
    r0 = row0 // tm
    return pl.pallas_call(
        functools.partial(_ple_kernel, m=m, tm=tm),
        grid=(pl.cdiv(n, tn), pl.cdiv(m, tm)),
        in_specs=[pl.BlockSpec((tm, k), lambda j, i: (r0 + i, 0)),
                  pl.BlockSpec((None, k, tn), lambda j, i: (layer, 0, j)),
                  pl.BlockSpec((tm, kp), lambda j, i: (i, 0)),
                  pl.BlockSpec((None, kp, tn), lambda j, i: (layer, 0, j)),
                  pl.BlockSpec((tm, tn), lambda j, i: (r0 + i, j))],
        out_specs=pl.BlockSpec((tm, tn), lambda j, i: (i, j)),
        out_shape=jax.ShapeDtypeStruct((m, n), F32),
        scratch_shapes=[pltpu.VMEM((k, tn), BF16), pltpu.VMEM((kp, tn), BF16)],
        compiler_params=_cparams(("parallel", "arbitrary")),
        name="ple",
    )(a, wg, p, wp, h)


def _mlstm_kernel(zq_ref, zk_ref, zv_ref, zo_ref, gc_ref, gr_ref, c0_ref, n0_ref, m0_ref, hg_ref,
                  hm_ref, c1_ref, n1_ref, m1_ref, ct_sc, n_sc, m_sc, *, n_heads, chunk):
    c = pl.program_id(1)
    last = pl.num_programs(1) - 1
    L = chunk
    scale = HEAD_DIM ** -0.5

    @pl.when(c == 0)
    def _():
        for h in range(n_heads):
            ct_sc[h] = c0_ref[h].T
        n_sc[...] = n0_ref[...]
        m_sc[...] = m0_ref[...]

    row = lax.broadcasted_iota(jnp.int32, (L, L), 0)
    col = lax.broadcasted_iota(jnp.int32, (L, L), 1)
    causal = col <= row
    tri_lo = jnp.where(causal, 1.0, 0.0).astype(BF16)
    tri_up = jnp.where(row <= col, 1.0, 0.0).astype(BF16)

    gc = gc_ref[...]
    gr = gr_ref[...]
    f_cols = _dot_hp_exact_lhs(tri_lo, _log_sigmoid(gc), _dot_nn)
    f_rows = _dot_hp_exact_rhs(_log_sigmoid(gr), tri_up, _dot_nn)

    for h in range(n_heads):
        lanes = slice(h * HEAD_DIM, (h + 1) * HEAD_DIM)
        q = zq_ref[:, lanes]
        k = zk_ref[:, lanes] * scale
        v = zv_ref[:, lanes]
        fc = f_cols[:, n_heads + h:n_heads + h + 1]
        fr = f_rows[n_heads + h:n_heads + h + 1, :]
        ic = gc[:, h:h + 1]
        ir = gr[h:h + 1, :]
        m_prev = m_sc[h:h + 1, 0:1]
        n_prev = n_sc[h:h + 1, :]
        ct_prev = ct_sc[h]

        dmat = jnp.where(causal, fc - fr + ir, NEG)
        inter = fc + m_prev
        m_t = jnp.maximum(inter, jnp.max(dmat, axis=-1, keepdims=True))
        w_intra = jnp.exp(dmat - m_t)
        w_inter = jnp.exp(inter - m_t)
        qb, kb, vb = q.astype(BF16), k.astype(BF16), v.astype(BF16)
        a = w_intra * _dot_nt(qb, kb)
        num = (w_inter * jnp.dot(qb, ct_prev.astype(BF16), preferred_element_type=F32)
               + jnp.dot(a.astype(BF16), vb, preferred_element_type=F32))
        den = w_inter * jnp.sum(q * n_prev, axis=-1, keepdims=True) + jnp.sum(a, axis=-1, keepdims=True)
        hval = num / jnp.maximum(jnp.abs(den), jnp.exp(-m_t))

        m_end = m_t[L - 1:L, :]
        w_s = jnp.exp(fc[L - 1:L, :] - fc + ic - m_end)
        decay = jnp.exp(inter[L - 1:L, :] - m_end)
        kw = w_s * k
        ct_new = decay * ct_prev + _dot_tn(kw.astype(BF16), vb)
        n_new = decay * n_prev + jnp.sum(kw, axis=0, keepdims=True)
        ct_sc[h] = ct_new
        n_sc[h:h + 1, :] = n_new
        m_sc[h:h + 1, :] = jnp.broadcast_to(m_end, (1, HEAD_DIM))

        hn = _rms_rows(hval, hg_ref[:, lanes])
        hm_ref[:, lanes] = (hn * _sigmoid(zo_ref[:, lanes])).astype(hm_ref.dtype)

    @pl.when(c == last)
    def _():
        for h in range(n_heads):
            c1_ref[h] = ct_sc[h].T
        n1_ref[...] = n_sc[...]
        m1_ref[...] = m_sc[...]


def mlstm(zz, gates_c, gates_r, c0, n0, m0, h_norm_g, *, batch, seq, chunk):
    n_heads = c0.shape[1]
    dm = n_heads * HEAD_DIM
    nc = seq // chunk
    m0b = jnp.broadcast_to(m0[:, :, None], (batch, n_heads, HEAD_DIM))
    zspec = lambda g: pl.BlockSpec((chunk, dm), lambda b, c, g=g: (b * nc + c, g))
    state = lambda *shape: pl.BlockSpec((None,) + shape, lambda b, c: (b,) + (0,) * len(shape))
    hm, c1, n1, m1 = pl.pallas_call(
        functools.partial(_mlstm_kernel, n_heads=n_heads, chunk=chunk),
        grid=(batch, nc),
        in_specs=[zspec(0), zspec(1), zspec(2), zspec(3),
                  pl.BlockSpec((chunk, gates_c.shape[1]), lambda b, c: (b * nc + c, 0)),
                  pl.BlockSpec((None, 2 * n_heads, chunk), lambda b, c: (b, 0, c)),
                  state(n_heads, HEAD_DIM, HEAD_DIM), state(n_heads, HEAD_DIM), state(n_heads, HEAD_DIM),
                  pl.BlockSpec((1, dm), lambda b, c: (0, 0))],
        out_specs=[pl.BlockSpec((chunk, dm), lambda b, c: (b * nc + c, 0)),
                   state(n_heads, HEAD_DIM, HEAD_DIM), state(n_heads, HEAD_DIM), state(n_heads, HEAD_DIM)],
        out_shape=[jax.ShapeDtypeStruct((batch * seq, dm), BF16),
                   jax.ShapeDtypeStruct((batch, n_heads, HEAD_DIM, HEAD_DIM), F32),
                   jax.ShapeDtypeStruct((batch, n_heads, HEAD_DIM), F32),
                   jax.ShapeDtypeStruct((batch, n_heads, HEAD_DIM), F32)],
        scratch_shapes=[pltpu.VMEM((n_heads, HEAD_DIM, HEAD_DIM), F32),
                        pltpu.VMEM((n_heads, HEAD_DIM), F32),
                        pltpu.VMEM((n_heads, HEAD_DIM), F32)],
        compiler_params=_cparams(("parallel", "arbitrary")),
        name="mlstm",
    )(zz, zz, zz, zz, gates_c, gates_r, c0, n0, m0b, h_norm_g.reshape(1, dm))
    return hm, c1, n1, m1[:, :, 0]


def _conv_kernel(cb_ref, cc_ref, cx_ref, s0_ref, w_ref, hc_ref, s1_ref, carry_sc, *, rows, last_row):
    t = pl.program_id(1)
    last = pl.num_programs(1) - 1

    @pl.when(t == 0)
    def _():
        carry_sc[...] = s0_ref[...]

    u = cc_ref[...] * cx_ref[...]
    prev2 = carry_sc[0:1, :]
    prev1 = carry_sc[1:2, :]
    r = lax.broadcasted_iota(jnp.int32, u.shape, 0)
    u1 = jnp.where(r == 0, prev1, pltpu.roll(u, 1, 0))
    u2 = jnp.where(r == 0, prev2, jnp.where(r == 1, prev1, pltpu.roll(u, 2, 0)))
    y = w_ref[0:1, :] * u2 + w_ref[1:2, :] * u1 + w_ref[2:3, :] * u
    hc_ref[...] = (cb_ref[...] * y).astype(hc_ref.dtype)
    carry_sc[...] = u[rows - 2:rows, :]

    @pl.when(t == last)
    def _():
        s1_ref[...] = u[last_row - 1:last_row + 1, :]


def short_conv(zz, col0, s0, w, *, batch, seq, rows, valid):
    c = s0.shape[-1]
    nt = seq // rows
    last_row = valid - 1 - (nt - 1) * rows
    zspec = lambda g: pl.BlockSpec((rows, c), lambda b, t, g=g: (b * nt + t, col0 + g))
    return pl.pallas_call(
        functools.partial(_conv_kernel, rows=rows, last_row=last_row),
        grid=(batch, nt),
        in_specs=[zspec(0), zspec(1), zspec(2),
                  pl.BlockSpec((None, 2, c), lambda b, t: (b, 0, 0)),
                  pl.BlockSpec((3, c), lambda b, t: (0, 0))],
        out_specs=[pl.BlockSpec((rows, c), lambda b, t: (b * nt + t, 0)),
                   pl.BlockSpec((None, 2, c), lambda b, t: (b, 0, 0))],
        out_shape=[jax.ShapeDtypeStruct((batch * seq, c), BF16),
                   jax.ShapeDtypeStruct((batch, 2, c), F32)],
        scratch_shapes=[pltpu.VMEM((2, c), F32)],
        compiler_params=_cparams(("parallel", "arbitrary")),
        name="short_conv",
    )(zz, zz, zz, s0, w)


def _top_mask_rows(gate, valid, n_keep):
    n = gate.shape[0]
    jidx = lax.broadcasted_iota(jnp.int32, gate.shape, 0)
    g = jnp.where(valid, gate, NEG)
    rank = jnp.zeros(gate.shape, F32)
    for j in range(n):
        gj = g[j:j + 1, :]
        beats = (gj > g) | ((gj == g) & (jidx > j))
        rank = rank + jnp.where(beats, 1.0, 0.0)
    return jnp.where(valid & (rank < n_keep), 1.0, 0.0)


def _moba_prompt_kernel(q_ref, k_ref, v_ref, gq_ref, gk_ref, ko_ref, vo_ref, ha_ref, *, seq):
    nb = seq // MOBA_BLOCK
    scale = HEAD_DIM ** -0.5
    qn = _rms_rows(q_ref[...], gq_ref[...])
    kn = _rms_rows(k_ref[...], gk_ref[...])
    v = v_ref[...]
    ko_ref[...] = kn
    vo_ref[...] = v
    qb, kb, vb = qn.astype(BF16), kn.astype(BF16), v.astype(BF16)

    avg = jnp.where(lax.broadcasted_iota(jnp.int32, (HEAD_DIM, seq), 1) // MOBA_BLOCK
                    == lax.broadcasted_iota(jnp.int32, (HEAD_DIM, seq), 0), 1.0 / MOBA_BLOCK, 0.0).astype(BF16)
    k_mean = _dot_hp_exact_lhs(avg, kn, _dot_nn)
    gate_t = _dot_3x(k_mean, qn, _dot_nt)
    own = lax.broadcasted_iota(jnp.int32, gate_t.shape, 1) // MOBA_BLOCK
    blk = lax.broadcasted_iota(jnp.int32, gate_t.shape, 0)
    sel = _top_mask_rows(gate_t[:8 * pl.cdiv(nb, 8)], (blk < own)[:8 * pl.cdiv(nb, 8)], MOBA_TOPK)
    sel = jnp.concatenate([sel, jnp.zeros((HEAD_DIM - sel.shape[0], seq), F32)], axis=0).T

    r = lax.broadcasted_iota(jnp.int32, (MOBA_BLOCK, MOBA_BLOCK), 0)
    c = lax.broadcasted_iota(jnp.int32, (MOBA_BLOCK, MOBA_BLOCK), 1)
    for i in range(nb):
        rows = slice(i * MOBA_BLOCK, (i + 1) * MOBA_BLOCK)
        kv = (i + 1) * MOBA_BLOCK
        s = _dot_nt(qb[rows], kb[:kv]) * scale
        parts = [jnp.where(sel[rows, j:j + 1] > 0.5, s[:, j * MOBA_BLOCK:(j + 1) * MOBA_BLOCK], NEG)
                 for j in range(i)]
        parts.append(jnp.where(c <= r, s[:, i * MOBA_BLOCK:], NEG))
        s = parts[0] if i == 0 else jnp.concatenate(parts, axis=1)
        p = jnp.exp(s - jnp.max(s, axis=-1, keepdims=True))
        o = jnp.dot(p.astype(BF16), vb[:kv], preferred_element_type=F32) / jnp.sum(p, axis=-1, keepdims=True)
        ha_ref[rows, :] = o.astype(ha_ref.dtype)


def moba_prompt(zz, colq, gq, gk, *, batch, seq, n_heads):
    d = HEAD_DIM
    zspec = lambda g: pl.BlockSpec((seq, d), lambda b, h, g=g: (b, colq + g * n_heads + h))
    ospec = pl.BlockSpec((None, seq, d), lambda b, h: (b, 0, h))
    ko, vo, ha = pl.pallas_call(
        functools.partial(_moba_prompt_kernel, seq=seq),
        grid=(batch, n_heads),
        in_specs=[zspec(0), zspec(1), zspec(2),
                  pl.BlockSpec((1, d), lambda b, h: (0, 0)),
                  pl.BlockSpec((1, d), lambda b, h: (0, 0))],
        out_specs=[ospec, ospec, pl.BlockSpec((seq, d), lambda b, h: (b, h))],
        out_shape=[jax.ShapeDtypeStruct((batch, seq, n_heads * d), F32),
                   jax.ShapeDtypeStruct((batch, seq, n_heads * d), F32),
                   jax.ShapeDtypeStruct((batch * seq, n_heads * d), BF16)],
        compiler_params=_cparams(("parallel", "parallel")),
        name="moba_prompt",
    )(zz, zz, zz, gq.reshape(1, d), gk.reshape(1, d))
    return ha, ko, vo


PAGES_PER_BLOCK = MOBA_BLOCK // PAGE_SIZE
BLOCKS_PER_STEP = 4


def _moba_pages_kernel(pt_ref, q_ref, *refs, n_heads, group):
    del pt_ref
    n_pg = group * PAGES_PER_BLOCK
    k_refs, v_refs = refs[:n_pg], refs[n_pg:2 * n_pg]
    gq_ref, km_ref, ms_ref, ls_ref, op_ref, q_sc = refs[2 * n_pg:]
    j = pl.program_id(1)
    scale = HEAD_DIM ** -0.5
    rows = q_sc.shape[0]

    @pl.when(j == 0)
    def _():
        q_sc[...] = _rms_rows(q_ref[...], gq_ref[...]).astype(BF16)
        ms_ref[...] = jnp.zeros(ms_ref.shape, F32)
        ls_ref[...] = jnp.zeros(ls_ref.shape, F32)

    flat = lambda r: r[...].reshape(PAGE_SIZE * n_heads, HEAD_DIM).astype(BF16)
    lane = lax.broadcasted_iota(jnp.int32, (rows, ms_ref.shape[-1]), 1)
    for u in range(group):
        pages = range(u * PAGES_PER_BLOCK, (u + 1) * PAGES_PER_BLOCK)
        km_ref[u] = sum(jnp.sum(k_refs[pg][...], axis=0) for pg in pages) * (1.0 / MOBA_BLOCK)
        kb = jnp.concatenate([flat(k_refs[pg]) for pg in pages], axis=0)
        vb = jnp.concatenate([flat(v_refs[pg]) for pg in pages], axis=0)
        s = _dot_nt(q_sc[...], kb) * scale
        qhead = lax.broadcasted_iota(jnp.int32, s.shape, 0) % n_heads
        khead = lax.broadcasted_iota(jnp.int32, s.shape, 1) % n_heads
        s = jnp.where(qhead == khead, s, NEG)
        m = jnp.max(s, axis=-1, keepdims=True)
        p = jnp.exp(s - m)
        l = jnp.sum(p, axis=-1, keepdims=True)
        op_ref[u] = jnp.dot(p.astype(BF16), vb, preferred_element_type=F32)
        blk = j * group + u
        ms_ref[...] = jnp.where(lane == blk, m, ms_ref[...])
        ls_ref[...] = jnp.where(lane == blk, l, ls_ref[...])


def moba_pages(page_table, q, cache_k, cache_v, layer, gq, *, n_heads):
    batch, n_pages = page_table.shape
    nblk = n_pages // PAGES_PER_BLOCK
    group = math.gcd(nblk, BLOCKS_PER_STEP)
    n_pg = group * PAGES_PER_BLOCK
    d = HEAD_DIM
    rows = q.shape[1]
    stat_lanes = 128 * pl.cdiv(nblk, 128)
    page = lambda o: pl.BlockSpec((None, None, PAGE_SIZE, n_heads, d),
                                  lambda b, j, pt, o=o: (layer, pt[b * n_pages + n_pg * j + o], 0, 0, 0))
    grid_spec = pltpu.PrefetchScalarGridSpec(
        num_scalar_prefetch=1,
        grid=(batch, nblk // group),
        in_specs=[pl.BlockSpec((None, rows, d), lambda b, j, pt: (b, 0, 0))]
                 + [page(o) for o in range(n_pg)] + [page(o) for o in range(n_pg)]
                 + [pl.BlockSpec((1, d), lambda b, j, pt: (0, 0))],
        out_specs=[pl.BlockSpec((None, group, n_heads, d), lambda b, j, pt: (b, j, 0, 0)),
                   pl.BlockSpec((None, rows, stat_lanes), lambda b, j, pt: (b, 0, 0)),
                   pl.BlockSpec((None, rows, stat_lanes), lambda b, j, pt: (b, 0, 0)),
                   pl.BlockSpec((None, group, rows, d), lambda b, j, pt: (b, j, 0, 0))],
        scratch_shapes=[pltpu.VMEM((rows, d), BF16)],
    )
    return pl.pallas_call(
        functools.partial(_moba_pages_kernel, n_heads=n_heads, group=group),
        grid_spec=grid_spec,
        out_shape=[jax.ShapeDtypeStruct((batch, nblk, n_heads, d), F32),
                   jax.ShapeDtypeStruct((batch, rows, stat_lanes), F32),
                   jax.ShapeDtypeStruct((batch, rows, stat_lanes), F32),
                   jax.ShapeDtypeStruct((batch, nblk, rows, d), F32)],
        compiler_params=_cparams(("parallel", "arbitrary")),
        name="moba_pages",
    )(page_table.reshape(-1), q, *([cache_k] * n_pg), *([cache_v] * n_pg), gq.reshape(1, d))


def _moba_combine_kernel(q_ref, kn_ref, vn_ref, km_ref, ms_ref, ls_ref, op_ref, gq_ref, gk_ref,
                         ha_ref, ko_ref, *, n_heads, nblk):
    scale = HEAD_DIM ** -0.5
    q = _rms_rows(q_ref[...], gq_ref[...])
    kn = _rms_rows(kn_ref[...], gk_ref[...])
    ko_ref[...] = kn

    km = km_ref[...].reshape(nblk * n_heads, HEAD_DIM)
    g_all = _dot_hp(q, km, _dot_nt)
    rhead = lax.broadcasted_iota(jnp.int32, g_all.shape, 0) % n_heads
    chead = lax.broadcasted_iota(jnp.int32, g_all.shape, 1) % n_heads
    g_all = jnp.where(rhead == chead, g_all, 0.0)
    pick = jnp.where(lax.broadcasted_iota(jnp.int32, (nblk * n_heads, nblk), 0) // n_heads
                     == lax.broadcasted_iota(jnp.int32, (nblk * n_heads, nblk), 1), 1.0, 0.0).astype(BF16)
    gate = _dot_hp_exact_rhs(g_all, pick, _dot_nn)

    lane = lax.broadcasted_iota(jnp.int32, gate.shape, 1)
    rank = jnp.zeros(gate.shape, F32)
    for j in range(nblk):
        gj = gate[:, j:j + 1]
        beats = (gj > gate) | ((gj == gate) & (lane > j))
        rank = rank + jnp.where(beats, 1.0, 0.0)
    sel = rank < min(MOBA_TOPK, nblk)

    s_own = _dot_nt(q.astype(BF16), kn.astype(BF16)) * scale
    r = lax.broadcasted_iota(jnp.int32, s_own.shape, 0)
    c = lax.broadcasted_iota(jnp.int32, s_own.shape, 1)
    s_own = jnp.where((r % n_heads == c % n_heads) & (c // n_heads <= r // n_heads), s_own, NEG)

    ms = ms_ref[:, :nblk]
    ls = ls_ref[:, :nblk]
    m_star = jnp.maximum(jnp.max(jnp.where(sel, ms, NEG), axis=-1, keepdims=True),
                         jnp.max(s_own, axis=-1, keepdims=True))
    w = jnp.where(sel, jnp.exp(ms - m_star), 0.0)
    p_own = jnp.exp(s_own - m_star)
    l = jnp.sum(w * ls, axis=-1, keepdims=True) + jnp.sum(p_own, axis=-1, keepdims=True)
    o = jnp.dot(p_own.astype(BF16), vn_ref[...].astype(BF16), preferred_element_type=F32)
    for j in range(nblk):
        o = o + w[:, j:j + 1] * op_ref[j]
    ha_ref[...] = (o / l).astype(ha_ref.dtype)


def moba_combine(q, kn, vn, k_mean, ms, ls, op, gq, gk, *, n_heads):
    batch, nblk = k_mean.shape[:2]
    d = HEAD_DIM
    rows = q.shape[1]
    full = lambda a: pl.BlockSpec((None,) + a.shape[1:], lambda b: (b,) + (0,) * (a.ndim - 1))
    vec = pl.BlockSpec((1, d), lambda b: (0, 0))
    return pl.pallas_call(
        functools.partial(_moba_combine_kernel, n_heads=n_heads, nblk=nblk),
        grid=(batch,),
        in_specs=[full(q), full(kn), full(vn), full(k_mean), full(ms), full(ls), full(op), vec, vec],
        out_specs=[pl.BlockSpec((None, rows, d), lambda b: (b, 0, 0)),
                   pl.BlockSpec((None, rows, d), lambda b: (b, 0, 0))],
        out_shape=[jax.ShapeDtypeStruct((batch, rows, d), BF16),
                   jax.ShapeDtypeStruct((batch, rows, d), F32)],
        compiler_params=_cparams(("parallel",)),
        name="moba_combine",
    )(q, kn, vn, k_mean, ms, ls, op, gq.reshape(1, d), gk.reshape(1, d))


CONV_PAD = 8


def _pad_rows(x, rows):
    return jnp.pad(x, ((0, 0), (0, rows - x.shape[1]), (0, 0)))


def kernel(x_prompt, x_sample, cache_k, cache_v, state_mlstm_c, state_mlstm_n, state_mlstm_m, state_conv,
           page_table, p_prompt, p_sample, g_ffn1, w1_a, w3_a, w2_a, g_mix, w_in, b_if, q_norm_g, k_norm_g,
           h_norm_g, conv_w, w_out, g_ffn2, w1_b, w3_b, w2_b, g_ple, w_pgate, w_ple):
    bp, seq, d_model = x_prompt.shape
    bs, t_new, _ = x_sample.shape
    depth = g_ffn1.shape[0]
    h_m = state_mlstm_c.shape[2]
    h_a = cache_k.shape[3]
    d_m, d_a = h_m * HEAD_DIM, h_a * HEAD_DIM
    d_c = d_model - d_m - d_a
    n_p, n_s = bp * seq, bs * t_new
    assert 2 <= t_new <= CONV_PAD and (t_new * h_a) % 8 == 0
    assert seq % MLSTM_CHUNK == 0 and seq % MOBA_BLOCK == 0

    g0 = 4 * d_m
    g1 = g0 + 2 * h_m
    n_gate_lanes = 128
    col_aq = g0 // HEAD_DIM
    col_cb = (g0 + 3 * d_a) // d_c

    h = jnp.concatenate([x_prompt.reshape(n_p, d_model), x_sample.reshape(n_s, d_model)], axis=0)
    outs = {k: [] for k in ("kp", "vp", "ks", "vs", "cp", "np", "mp", "cs", "ns", "ms", "cvp", "cvs")}
    s_chunk = 128

    ffn_split = 2 if (w2_a.shape[1] // 2) % 128 == 0 else 1
    w_in_t = jnp.swapaxes(w_in, 1, 2)
    for l in range(depth):
        g = gate_up(rmsnorm_bf16(h, g_ffn1[l]), w1_a, w3_a, l)
        h = mm_resid(g, w2_a, l, h, alpha=0.5, max_tile=ROW_TILE_DOWN, k_split=ffn_split)

        b_gate = jnp.pad(b_if[l], (0, n_gate_lanes - 2 * h_m)).reshape(1, n_gate_lanes)
        u, gates = rmsnorm_gates(h, g_mix[l], w_in_t, l, g0, b_gate)
        z = in_proj(u, w_in_t, l, n_head=g0, shift=2 * h_m)

        gr_p = gates[:n_p, :2 * h_m].reshape(bp, seq, 2 * h_m).transpose(0, 2, 1)
        hm_p, c_p, nn_p, m_p = mlstm(
            z, gates, gr_p, jnp.zeros((bp, h_m, HEAD_DIM, HEAD_DIM), F32), jnp.zeros((bp, h_m, HEAD_DIM), F32),
            jnp.zeros((bp, h_m), F32), h_norm_g[l], batch=bp, seq=seq, chunk=MLSTM_CHUNK)
        ha_p, k_p, v_p = moba_prompt(z, col_aq, q_norm_g[l], k_norm_g[l], batch=bp, seq=seq, n_heads=h_a)
        hc_p, cv_p = short_conv(z, col_cb, jnp.zeros((bp, 2, d_c), F32), conv_w[l],
                                batch=bp, seq=seq, rows=512, valid=seq)

        zs = z[n_p:].reshape(bs, t_new, -1)
        gs = gates[n_p:].reshape(bs, t_new, n_gate_lanes)
        pad_gate = jnp.concatenate([jnp.full((h_m,), NEG, F32), jnp.full((h_m,), -NEG, F32),
                                    jnp.zeros((n_gate_lanes - 2 * h_m,), F32)])
        gs_pad = jnp.concatenate([gs, jnp.broadcast_to(pad_gate, (bs, s_chunk - t_new, n_gate_lanes))], axis=1)
        hm_s, c_s, nn_s, m_s = mlstm(
            _pad_rows(zs[:, :, :g0], s_chunk).reshape(bs * s_chunk, g0), gs_pad.reshape(bs * s_chunk, n_gate_lanes),
            gs_pad[:, :, :2 * h_m].transpose(0, 2, 1), state_mlstm_c[l], state_mlstm_n[l], state_mlstm_m[l],
            h_norm_g[l], batch=bs, seq=s_chunk, chunk=s_chunk)
        hm_s = hm_s.reshape(bs, s_chunk, d_m)[:, :t_new].reshape(n_s, d_m)

        heads = lambda a: a.reshape(bs, t_new * h_a, HEAD_DIM)
        q_s = heads(zs[:, :, g0:g0 + d_a])
        kn_s = heads(zs[:, :, g0 + d_a:g0 + 2 * d_a])
        vn_s = heads(zs[:, :, g0 + 2 * d_a:g0 + 3 * d_a])
        k_mean, ms, ls, op = moba_pages(page_table, q_s, cache_k, cache_v, l, q_norm_g[l], n_heads=h_a)
        ha_s, k_s = moba_combine(q_s, kn_s, vn_s, k_mean, ms, ls, op, q_norm_g[l], k_norm_g[l], n_heads=h_a)
        ha_s = ha_s.reshape(n_s, d_a)
        k_s = k_s.reshape(bs, t_new, h_a, HEAD_DIM)
        v_s = vn_s.reshape(bs, t_new, h_a, HEAD_DIM)

        zc = _pad_rows(zs[:, :, g0 + 3 * d_a:], CONV_PAD).reshape(bs * CONV_PAD, 3 * d_c)
        hc_s, cv_s = short_conv(zc, 0, state_conv[l], conv_w[l], batch=bs, seq=CONV_PAD, rows=CONV_PAD,
                                valid=t_new)
        hc_s = hc_s.reshape(bs, CONV_PAD, d_c)[:, :t_new].reshape(n_s, d_c)

        mix = jnp.concatenate([jnp.concatenate([hm_p, ha_p, hc_p], axis=1),
                               jnp.concatenate([hm_s, ha_s, hc_s], axis=1)], axis=0)
        h = mm_resid(mix, w_out, l, h, alpha=1.0, max_tile=ROW_TILE)

        g = gate_up(rmsnorm_bf16(h, g_ffn2[l]), w1_b, w3_b, l)
        h = mm_resid(g, w2_b, l, h, alpha=0.5, max_tile=ROW_TILE_DOWN, k_split=ffn_split)

        n4 = rmsnorm_bf16(h, g_ple[l])
        pp, ps = p_prompt[l].reshape(n_p, -1), p_sample[l].reshape(n_s, -1)
        if l < depth - 1:
            h = ple(n4, w_pgate, jnp.concatenate([pp, ps], axis=0), w_ple, l, h)
        else:
            y_p = ple(n4, w_pgate, pp, w_ple, l, h, row0=0, rows=n_p)
            y_s = ple(n4, w_pgate, ps, w_ple, l, h, row0=n_p, rows=n_s)

        outs["kp"].append(k_p.reshape(bp, seq, h_a, HEAD_DIM))
        outs["vp"].append(v_p.reshape(bp, seq, h_a, HEAD_DIM))
        outs["ks"].append(k_s)
        outs["vs"].append(v_s)
        outs["cp"].append(c_p); outs["np"].append(nn_p); outs["mp"].append(m_p)
        outs["cs"].append(c_s); outs["ns"].append(nn_s); outs["ms"].append(m_s)
        outs["cvp"].append(cv_p); outs["cvs"].append(cv_s)

    st = lambda k: jnp.stack(outs[k])
    return (y_p.reshape(bp, seq, d_model), y_s.reshape(bs, t_new, d_model),
            st("kp"), st("vp"), st("ks"), st("vs"),
            st("cp"), st("np"), st("mp"), st("cs"), st("ns"), st("ms"), st("cvp"), st("cvs"))
```

```python
import functools
import math

import jax
import jax.numpy as jnp
from jax import lax
from jax.experimental import pallas as pl
from jax.experimental.pallas import tpu as pltpu

F32 = jnp.float32
BF16 = jnp.bfloat16

HEAD_DIM = 128
PAGE_SIZE = 128
MOBA_BLOCK = 256
MOBA_TOPK = 3
RMS_EPS = 1e-6
NEG = -1e30
MLSTM_CHUNK = 256
V7X_VMEM_LIMIT = 60000 * 1024


def _cparams(sem, vmem=V7X_VMEM_LIMIT):
    return pltpu.CompilerParams(dimension_semantics=sem, vmem_limit_bytes=vmem)


def _split3(x):
    hi = x.astype(BF16)
    r1 = x - hi.astype(F32)
    mid = r1.astype(BF16)
    lo = (r1 - mid.astype(F32)).astype(BF16)
    return hi, mid, lo


def _dot_nn(a, b):
    return jnp.dot(a, b, preferred_element_type=F32)


def _dot_nt(a, b):
    return lax.dot_general(a, b, (((1,), (1,)), ((), ())), preferred_element_type=F32)


def _dot_tn(a, b):
    return lax.dot_general(a, b, (((0,), (0,)), ((), ())), preferred_element_type=F32)


def _dot_hp(a, b, dot):
    a3, b3 = _split3(a), _split3(b)
    out = None
    for i, j in ((2, 0), (0, 2), (1, 1), (1, 0), (0, 1), (0, 0)):
        t = dot(a3[i], b3[j])
        out = t if out is None else out + t
    return out


def _dot_3x(a, b, dot):
    a_hi, a_mid, _ = _split3(a)
    b_hi, b_mid, _ = _split3(b)
    return dot(a_mid, b_hi) + dot(a_hi, b_mid) + dot(a_hi, b_hi)


def _dot_hp_exact_lhs(a_bf16, b, dot):
    b3 = _split3(b)
    return dot(a_bf16, b3[2]) + dot(a_bf16, b3[1]) + dot(a_bf16, b3[0])


def _dot_hp_exact_rhs(a, b_bf16, dot):
    a3 = _split3(a)
    return dot(a3[2], b_bf16) + dot(a3[1], b_bf16) + dot(a3[0], b_bf16)


def _rms_rows(x, g):
    return x * lax.rsqrt(jnp.mean(x * x, axis=-1, keepdims=True) + RMS_EPS) * g


def _log_sigmoid(x):
    return jnp.minimum(x, 0.0) - jnp.log1p(jnp.exp(-jnp.abs(x)))


def _sigmoid(x):
    return 1.0 / (1.0 + jnp.exp(-x))


def _rmsnorm_kernel(x_ref, g_ref, o_ref):
    o_ref[...] = _rms_rows(x_ref[...], g_ref[...]).astype(o_ref.dtype)


def _rmsnorm_gates_kernel(x_ref, g_ref, wg_ref, bg_ref, o_ref, gates_ref):
    y = _rms_rows(x_ref[...], g_ref[...])
    o_ref[...] = y.astype(o_ref.dtype)
    gates_ref[...] = _dot_3x(y, wg_ref[...], _dot_nt) + bg_ref[...]


def rmsnorm_bf16(x, g, *, tm=512):
    m, d = x.shape
    return pl.pallas_call(
        _rmsnorm_kernel,
        grid=(pl.cdiv(m, tm),),
        in_specs=[pl.BlockSpec((tm, d), lambda i: (i, 0)),
                  pl.BlockSpec((1, d), lambda i: (0, 0))],
        out_specs=pl.BlockSpec((tm, d), lambda i: (i, 0)),
        out_shape=jax.ShapeDtypeStruct((m, d), BF16),
        compiler_params=_cparams(("parallel",)),
        name="rmsnorm",
    )(x, g.reshape(1, d))


def rmsnorm_gates(x, g, wt, layer, col0, bg, *, tm=512):
    m, d = x.shape
    ng = bg.shape[1]
    assert col0 % ng == 0
    return pl.pallas_call(
        _rmsnorm_gates_kernel,
        grid=(pl.cdiv(m, tm),),
        in_specs=[pl.BlockSpec((tm, d), lambda i: (i, 0)),
                  pl.BlockSpec((1, d), lambda i: (0, 0)),
                  pl.BlockSpec((None, ng, d), lambda i: (layer, col0 // ng, 0)),
                  pl.BlockSpec((1, ng), lambda i: (0, 0))],
        out_specs=[pl.BlockSpec((tm, d), lambda i: (i, 0)),
                   pl.BlockSpec((tm, ng), lambda i: (i, 0))],
        out_shape=[jax.ShapeDtypeStruct((m, d), BF16),
                   jax.ShapeDtypeStruct((m, ng), F32)],
        compiler_params=_cparams(("parallel",)),
        name="rmsnorm_gates",
    )(x, g.reshape(1, d), wt, bg)


BF16_SUBLANES = 16
ROW_TILE = 1040
ROW_TILE_GATE_UP = 1648
ROW_TILE_IN = 1376
ROW_TILE_DOWN = 832


def _row_tile(m, max_tile):
    n = pl.cdiv(m, max_tile)
    return BF16_SUBLANES * pl.cdiv(pl.cdiv(m, n), BF16_SUBLANES)


def _row_cases(m, tm, body):
    nm = pl.cdiv(m, tm)
    rem = m - (nm - 1) * tm
    if rem == tm:
        body(tm)
        return
    i = pl.program_id(1)

    @pl.when(i < nm - 1)
    def _():
        body(tm)

    @pl.when(i == nm - 1)
    def _():
        body(rem)


def _cast_weights_once(pairs):
    @pl.when(pl.program_id(1) == 0)
    def _():
        for w_ref, w_sc in pairs:
            w_sc[...] = w_ref[...].astype(BF16)


def _gate_up_kernel(a_ref, w1_ref, w3_ref, o_ref, w1_sc, w3_sc, *, m, tm):
    _cast_weights_once(((w1_ref, w1_sc), (w3_ref, w3_sc)))

    def body(rows):
        a = a_ref[:rows, :]
        x1 = jnp.dot(a, w1_sc[...], preferred_element_type=F32)
        x3 = jnp.dot(a, w3_sc[...], preferred_element_type=F32)
        o_ref[:rows, :] = (x1 * _sigmoid(x1) * x3).astype(o_ref.dtype)
    _row_cases(m, tm, body)


def gate_up(a, w1, w3, layer, *, tn=256):
    m, k = a.shape
    n = w1.shape[2]
    tm = _row_tile(m, ROW_TILE_GATE_UP)
    wspec = pl.BlockSpec((None, k, tn), lambda j, i: (layer, 0, j))
    return pl.pallas_call(
        functools.partial(_gate_up_kernel, m=m, tm=tm),
        grid=(pl.cdiv(n, tn), pl.cdiv(m, tm)),
        in_specs=[pl.BlockSpec((tm, k), lambda j, i: (i, 0)), wspec, wspec],
        out_specs=pl.BlockSpec((tm, tn), lambda j, i: (i, j)),
        out_shape=jax.ShapeDtypeStruct((m, n), BF16),
        scratch_shapes=[pltpu.VMEM((k, tn), BF16), pltpu.VMEM((k, tn), BF16)],
        compiler_params=_cparams(("parallel", "arbitrary")),
        name="gate_up",
    )(a, w1, w3)


def _mm_resid_kernel(a_ref, w_ref, h_ref, o_ref, w_sc, *, m, tm, alpha):
    _cast_weights_once(((w_ref, w_sc),))

    def body(rows):
        acc = jnp.dot(a_ref[:rows, :], w_sc[...], preferred_element_type=F32)
        o_ref[:rows, :] = h_ref[:rows, :] + alpha * acc
    _row_cases(m, tm, body)


def mm_resid(a, w, layer, h, *, alpha, max_tile, tn=512, k_split=1):
    m, k = a.shape
    n = w.shape[2]
    tm = _row_tile(m, max_tile)
    kc = k // k_split
    assert kc * k_split == k and kc % 128 == 0
    for c in range(k_split):
        h = pl.pallas_call(
            functools.partial(_mm_resid_kernel, m=m, tm=tm, alpha=alpha),
            grid=(pl.cdiv(n, tn), pl.cdiv(m, tm)),
            in_specs=[pl.BlockSpec((tm, kc), lambda j, i, c=c: (i, c)),
                      pl.BlockSpec((None, kc, tn), lambda j, i, c=c: (layer, c, j)),
                      pl.BlockSpec((tm, tn), lambda j, i: (i, j))],
            out_specs=pl.BlockSpec((tm, tn), lambda j, i: (i, j)),
            out_shape=jax.ShapeDtypeStruct((m, n), F32),
            scratch_shapes=[pltpu.VMEM((kc, tn), BF16)],
            compiler_params=_cparams(("parallel", "arbitrary")),
            name="mm_resid",
        )(a, w, h)
    return h


def _in_proj_kernel(a_ref, w_ref, wx_ref, o_ref, w_sc, *, m, tm, n_head_tiles, shift):
    j = pl.program_id(0)
    tn = w_sc.shape[0]

    @pl.when((pl.program_id(1) == 0) & (j < n_head_tiles))
    def _():
        w_sc[...] = w_ref[...].astype(BF16)

    @pl.when((pl.program_id(1) == 0) & (j >= n_head_tiles))
    def _():
        w_sc[:tn - shift, :] = w_ref[shift:, :].astype(BF16)
        w_sc[tn - shift:, :] = wx_ref[...].astype(BF16)

    def body(rows):
        o_ref[:rows, :] = _dot_nt(a_ref[:rows, :], w_sc[...])
    _row_cases(m, tm, body)


def in_proj(a, wt, layer, *, n_head, shift, tn=512):
    m, k = a.shape
    n = wt.shape[1] - shift
    tm = _row_tile(m, ROW_TILE_IN)
    assert n % tn == 0 and n_head % tn == 0 and shift % 8 == 0 and tn % shift == 0
    return pl.pallas_call(
        functools.partial(_in_proj_kernel, m=m, tm=tm, n_head_tiles=n_head // tn, shift=shift),
        grid=(n // tn, pl.cdiv(m, tm)),
        in_specs=[pl.BlockSpec((tm, k), lambda j, i: (i, 0)),
                  pl.BlockSpec((None, tn, k), lambda j, i: (layer, j, 0)),
                  pl.BlockSpec((None, shift, k), lambda j, i: (layer, (j + 1) * (tn // shift), 0))],
        out_specs=pl.BlockSpec((tm, tn), lambda j, i: (i, j)),
        out_shape=jax.ShapeDtypeStruct((m, n), F32),
        scratch_shapes=[pltpu.VMEM((tn, k), BF16)],
        compiler_params=_cparams(("parallel", "arbitrary")),
        name="in_proj",
    )(a, wt, wt)


def _ple_kernel(a_ref, wg_ref, p_ref, wp_ref, h_ref, o_ref, wg_sc, wp_sc, *, m, tm):
    _cast_weights_once(((wg_ref, wg_sc), (wp_ref, wp_sc)))

    def body(rows):
        gate = _sigmoid(jnp.dot(a_ref[:rows, :], wg_sc[...], preferred_element_type=F32))
        pe = jnp.dot(p_ref[:rows, :].astype(BF16), wp_sc[...], preferred_element_type=F32)
        o_ref[:rows, :] = h_ref[:rows, :] + gate * pe
    _row_cases(m, tm, body)


def ple(a, wg, p, wp, layer, h, *, row0=0, rows=None, tn=512):
    k = a.shape[1]
    m = a.shape[0] if rows is None else rows
    tm = _row_tile(m, ROW_TILE)
    n = wg.shape[2]
    kp = p.shape[1]
    assert row0 % tm == 0
    r0 = row0 // tm
    return pl.pallas_call(
        functools.partial(_ple_kernel, m=m, tm=tm),
        grid=(pl.cdiv(n, tn), pl.cdiv(m, tm)),
        in_specs=[pl.BlockSpec((tm, k), lambda j, i: (r0 + i, 0)),
                  pl.BlockSpec((None, k, tn), lambda j, i: (layer, 0, j)),
                  pl.BlockSpec((tm, kp), lambda j, i: (i, 0)),
                  pl.BlockSpec((None, kp, tn), lambda j, i: (layer, 0, j)),
                  pl.BlockSpec((tm, tn), lambda j, i: (r0 + i, j))],
        out_specs=pl.BlockSpec((tm, tn), lambda j, i: (i, j)),
        out_shape=jax.ShapeDtypeStruct((m, n), F32),
        scratch_shapes=[pltpu.VMEM((k, tn), BF16), pltpu.VMEM((kp, tn), BF16)],
        compiler_params=_cparams(("parallel", "arbitrary")),
        name="ple",
    )(a, wg, p, wp, h)


def _mlstm_kernel(zq_ref, zk_ref, zv_ref, zo_ref, gc_ref, gr_ref, c0_ref, n0_ref, m0_ref, hg_ref,
                  hm_ref, c1_ref, n1_ref, m1_ref, ct_sc, n_sc, m_sc, *, n_heads, chunk):
    c = pl.program_id(1)
    last = pl.num_programs(1) - 1
    L = chunk
    scale = HEAD_DIM ** -0.5

    @pl.when(c == 0)
    def _():
        for h in range(n_heads):
            ct_sc[h] = c0_ref[h].T
        n_sc[...] = n0_ref[...]
        m_sc[...] = m0_ref[...]

    row = lax.broadcasted_iota(jnp.int32, (L, L), 0)
    col = lax.broadcasted_iota(jnp.int32, (L, L), 1)
    causal = col <= row
    tri_lo = jnp.where(causal, 1.0, 0.0).astype(BF16)
    tri_up = jnp.where(row <= col, 1.0, 0.0).astype(BF16)

    gc = gc_ref[...]
    gr = gr_ref[...]
    f_cols = _dot_hp_exact_lhs(tri_lo, _log_sigmoid(gc), _dot_nn)
    f_rows = _dot_hp_exact_rhs(_log_sigmoid(gr), tri_up, _dot_nn)

    for h in range(n_heads):
        lanes = slice(h * HEAD_DIM, (h + 1) * HEAD_DIM)
        q = zq_ref[:, lanes]
        k = zk_ref[:, lanes] * scale
        v = zv_ref[:, lanes]
        fc = f_cols[:, n_heads + h:n_heads + h + 1]
        fr = f_rows[n_heads + h:n_heads + h + 1, :]
        ic = gc[:, h:h + 1]
        ir = gr[h:h + 1, :]
        m_prev = m_sc[h:h + 1, 0:1]
        n_prev = n_sc[h:h + 1, :]
        ct_prev = ct_sc[h]

        dmat = jnp.where(causal, fc - fr + ir, NEG)
        inter = fc + m_prev
        m_t = jnp.maximum(inter, jnp.max(dmat, axis=-1, keepdims=True))
        w_intra = jnp.exp(dmat - m_t)
        w_inter = jnp.exp(inter - m_t)
        qb, kb, vb = q.astype(BF16), k.astype(BF16), v.astype(BF16)
        a = w_intra * _dot_nt(qb, kb)
        num = (w_inter * jnp.dot(qb, ct_prev.astype(BF16), preferred_element_type=F32)
               + jnp.dot(a.astype(BF16), vb, preferred_element_type=F32))
        den = w_inter * jnp.sum(q * n_prev, axis=-1, keepdims=True) + jnp.sum(a, axis=-1, keepdims=True)
        hval = num / jnp.maximum(jnp.abs(den), jnp.exp(-m_t))

        m_end = m_t[L - 1:L, :]
        w_s = jnp.exp(fc[L - 1:L, :] - fc + ic - m_end)
        decay = jnp.exp(inter[L - 1:L, :] - m_end)
        kw = w_s * k
        ct_new = decay * ct_prev + _dot_tn(kw.astype(BF16), vb)
        n_new = decay * n_prev + jnp.sum(kw, axis=0, keepdims=True)
        ct_sc[h] = ct_new
        n_sc[h:h + 1, :] = n_new
        m_sc[h:h + 1, :] = jnp.broadcast_to(m_end, (1, HEAD_DIM))

        hn = _rms_rows(hval, hg_ref[:, lanes])
        hm_ref[:, lanes] = (hn * _sigmoid(zo_ref[:, lanes])).astype(hm_ref.dtype)

    @pl.when(c == last)
    def _():
        for h in range(n_heads):
            c1_ref[h] = ct_sc[h].T
        n1_ref[...] = n_sc[...]
        m1_ref[...] = m_sc[...]


def mlstm(zz, gates_c, gates_r, c0, n0, m0, h_norm_g, *, batch, seq, chunk):
    n_heads = c0.shape[1]
    dm = n_heads * HEAD_DIM
    nc = seq // chunk
    m0b = jnp.broadcast_to(m0[:, :, None], (batch, n_heads, HEAD_DIM))
    zspec = lambda g: pl.BlockSpec((chunk, dm), lambda b, c, g=g: (b * nc + c, g))
    state = lambda *shape: pl.BlockSpec((None,) + shape, lambda b, c: (b,) + (0,) * len(shape))
    hm, c1, n1, m1 = pl.pallas_call(
        functools.partial(_mlstm_kernel, n_heads=n_heads, chunk=chunk),
        grid=(batch, nc),
        in_specs=[zspec(0), zspec(1), zspec(2), zspec(3),
                  pl.BlockSpec((chunk, gates_c.shape[1]), lambda b, c: (b * nc + c, 0)),
                  pl.BlockSpec((None, 2 * n_heads, chunk), lambda b, c: (b, 0, c)),
                  state(n_heads, HEAD_DIM, HEAD_DIM), state(n_heads, HEAD_DIM), state(n_heads, HEAD_DIM),
                  pl.BlockSpec((1, dm), lambda b, c: (0, 0))],
        out_specs=[pl.BlockSpec((chunk, dm), lambda b, c: (b * nc + c, 0)),
                   state(n_heads, HEAD_DIM, HEAD_DIM), state(n_heads, HEAD_DIM), state(n_heads, HEAD_DIM)],
        out_shape=[jax.ShapeDtypeStruct((batch * seq, dm), BF16),
                   jax.ShapeDtypeStruct((batch, n_heads, HEAD_DIM, HEAD_DIM), F32),
                   jax.ShapeDtypeStruct((batch, n_heads, HEAD_DIM), F32),
                   jax.ShapeDtypeStruct((batch, n_heads, HEAD_DIM), F32)],
        scratch_shapes=[pltpu.VMEM((n_heads, HEAD_DIM, HEAD_DIM), F32),
                        pltpu.VMEM((n_heads, HEAD_DIM), F32),
                        pltpu.VMEM((n_heads, HEAD_DIM), F32)],
        compiler_params=_cparams(("parallel", "arbitrary")),
        name="mlstm",
    )(zz, zz, zz, zz, gates_c, gates_r, c0, n0, m0b, h_norm_g.reshape(1, dm))
    return hm, c1, n1, m1[:, :, 0]


def _conv_kernel(cb_ref, cc_ref, cx_ref, s0_ref, w_ref, hc_ref, s1_ref, carry_sc, *, rows, last_row):
    t = pl.program_id(1)
    last = pl.num_programs(1) - 1

    @pl.when(t == 0)
    def _():
        carry_sc[...] = s0_ref[...]

    u = cc_ref[...] * cx_ref[...]
    prev2 = carry_sc[0:1, :]
    prev1 = carry_sc[1:2, :]
    r = lax.broadcasted_iota(jnp.int32, u.shape, 0)
    u1 = jnp.where(r == 0, prev1, pltpu.roll(u, 1, 0))
    u2 = jnp.where(r == 0, prev2, jnp.where(r == 1, prev1, pltpu.roll(u, 2, 0)))
    y = w_ref[0:1, :] * u2 + w_ref[1:2, :] * u1 + w_ref[2:3, :] * u
    hc_ref[...] = (cb_ref[...] * y).astype(hc_ref.dtype)
    carry_sc[...] = u[rows - 2:rows, :]

    @pl.when(t == last)
    def _():
        s1_ref[...] = u[last_row - 1:last_row + 1, :]


def short_conv(zz, col0, s0, w, *, batch, seq, rows, valid):
    c = s0.shape[-1]
    nt = seq // rows
    last_row = valid - 1 - (nt - 1) * rows
    zspec = lambda g: pl.BlockSpec((rows, c), lambda b, t, g=g: (b * nt + t, col0 + g))
    return pl.pallas_call(
        functools.partial(_conv_kernel, rows=rows, last_row=last_row),
        grid=(batch, nt),
        in_specs=[zspec(0), zspec(1), zspec(2),
                  pl.BlockSpec((None, 2, c), lambda b, t: (b, 0, 0)),
                  pl.BlockSpec((3, c), lambda b, t: (0, 0))],
        out_specs=[pl.BlockSpec((rows, c), lambda b, t: (b * nt + t, 0)),
                   pl.BlockSpec((None, 2, c), lambda b, t: (b, 0, 0))],
        out_shape=[jax.ShapeDtypeStruct((batch * seq, c), BF16),
                   jax.ShapeDtypeStruct((batch, 2, c), F32)],
        scratch_shapes=[pltpu.VMEM((2, c), F32)],
        compiler_params=_cparams(("parallel", "arbitrary")),
        name="short_conv",
    )(zz, zz, zz, s0, w)


def _top_mask_rows(gate, valid, n_keep):
    n = gate.shape[0]
    jidx = lax.broadcasted_iota(jnp.int32, gate.shape, 0)
    g = jnp.where(valid, gate, NEG)
    rank = jnp.zeros(gate.shape, F32)
    for j in range(n):
        gj = g[j:j + 1, :]
        beats = (gj > g) | ((gj == g) & (jidx > j))
        rank = rank + jnp.where(beats, 1.0, 0.0)
    return jnp.where(valid & (rank < n_keep), 1.0, 0.0)


def _moba_prompt_kernel(q_ref, k_ref, v_ref, gq_ref, gk_ref, *refs, seq, stacked):
    nb = seq // MOBA_BLOCK
    scale = HEAD_DIM ** -0.5
    qn = _rms_rows(q_ref[...], gq_ref[...])
    kn = _rms_rows(k_ref[...], gk_ref[...])
    v = v_ref[...]
    if stacked:
        k_prev_ref, v_prev_ref, ko_ref, vo_ref, ha_ref = refs
        ko_ref[0] = k_prev_ref[...]
        vo_ref[0] = v_prev_ref[...]
        ko_ref[1] = kn
        vo_ref[1] = v
    else:
        ko_ref, vo_ref, ha_ref = refs
        ko_ref[...] = kn
        vo_ref[...] = v
    qb, kb, vb = qn.astype(BF16), kn.astype(BF16), v.astype(BF16)

    avg = jnp.where(lax.broadcasted_iota(jnp.int32, (HEAD_DIM, seq), 1) // MOBA_BLOCK
                    == lax.broadcasted_iota(jnp.int32, (HEAD_DIM, seq), 0), 1.0 / MOBA_BLOCK, 0.0).astype(BF16)
    k_mean = _dot_hp_exact_lhs(avg, kn, _dot_nn)
    gate_t = _dot_3x(k_mean, qn, _dot_nt)
    own = lax.broadcasted_iota(jnp.int32, gate_t.shape, 1) // MOBA_BLOCK
    blk = lax.broadcasted_iota(jnp.int32, gate_t.shape, 0)
    sel = _top_mask_rows(gate_t[:8 * pl.cdiv(nb, 8)], (blk < own)[:8 * pl.cdiv(nb, 8)], MOBA_TOPK)
    sel = jnp.concatenate([sel, jnp.zeros((HEAD_DIM - sel.shape[0], seq), F32)], axis=0).T

    r = lax.broadcasted_iota(jnp.int32, (MOBA_BLOCK, MOBA_BLOCK), 0)
    c = lax.broadcasted_iota(jnp.int32, (MOBA_BLOCK, MOBA_BLOCK), 1)
    for i in range(nb):
        rows = slice(i * MOBA_BLOCK, (i + 1) * MOBA_BLOCK)
        kv = (i + 1) * MOBA_BLOCK
        s = _dot_nt(qb[rows], kb[:kv]) * scale
        parts = [jnp.where(sel[rows, j:j + 1] > 0.5, s[:, j * MOBA_BLOCK:(j + 1) * MOBA_BLOCK], NEG)
                 for j in range(i)]
        parts.append(jnp.where(c <= r, s[:, i * MOBA_BLOCK:], NEG))
        s = parts[0] if i == 0 else jnp.concatenate(parts, axis=1)
        p = jnp.exp(s - jnp.max(s, axis=-1, keepdims=True))
        o = jnp.dot(p.astype(BF16), vb[:kv], preferred_element_type=F32) / jnp.sum(p, axis=-1, keepdims=True)
        ha_ref[rows, :] = o.astype(ha_ref.dtype)


def moba_prompt(zz, colq, gq, gk, *, batch, seq, n_heads, prev=None):
    d = HEAD_DIM
    zspec = lambda g: pl.BlockSpec((seq, d), lambda b, h, g=g: (b, colq + g * n_heads + h))
    kvspec = pl.BlockSpec((None, seq, d), lambda b, h: (b, 0, h))
    vec = pl.BlockSpec((1, d), lambda b, h: (0, 0))
    kv_shape = jax.ShapeDtypeStruct((batch, seq, n_heads * d), F32)
    in_specs, args = [zspec(0), zspec(1), zspec(2), vec, vec], [zz, zz, zz, gq.reshape(1, d), gk.reshape(1, d)]
    ospec = kvspec
    if prev is not None:
        in_specs += [kvspec, kvspec]
        args += list(prev)
        ospec = pl.BlockSpec((2, None, seq, d), lambda b, h: (0, b, 0, h))
        kv_shape = jax.ShapeDtypeStruct((2, batch, seq, n_heads * d), F32)
    ko, vo, ha = pl.pallas_call(
        functools.partial(_moba_prompt_kernel, seq=seq, stacked=prev is not None),
        grid=(batch, n_heads),
        in_specs=in_specs,
        out_specs=[ospec, ospec, pl.BlockSpec((seq, d), lambda b, h: (b, h))],
        out_shape=[kv_shape, kv_shape, jax.ShapeDtypeStruct((batch * seq, n_heads * d), BF16)],
        compiler_params=_cparams(("parallel", "parallel")),
        name="moba_prompt",
    )(*args)
    return ha, ko, vo


PAGES_PER_BLOCK = MOBA_BLOCK // PAGE_SIZE
BLOCKS_PER_STEP = 4


def _moba_pages_kernel(pt_ref, q_ref, *refs, n_heads, group):
    del pt_ref
    n_pg = group * PAGES_PER_BLOCK
    k_refs, v_refs = refs[:n_pg], refs[n_pg:2 * n_pg]
    gq_ref, km_ref, ms_ref, ls_ref, op_ref, q_sc = refs[2 * n_pg:]
    j = pl.program_id(1)
    scale = HEAD_DIM ** -0.5
    rows = q_sc.shape[0]

    @pl.when(j == 0)
    def _():
        q_sc[...] = _rms_rows(q_ref[...], gq_ref[...]).astype(BF16)
        ms_ref[...] = jnp.zeros(ms_ref.shape, F32)
        ls_ref[...] = jnp.zeros(ls_ref.shape, F32)

    flat = lambda r: r[...].reshape(PAGE_SIZE * n_heads, HEAD_DIM).astype(BF16)
    lane = lax.broadcasted_iota(jnp.int32, (rows, ms_ref.shape[-1]), 1)
    for u in range(group):
        pages = range(u * PAGES_PER_BLOCK, (u + 1) * PAGES_PER_BLOCK)
        km_ref[u] = sum(jnp.sum(k_refs[pg][...], axis=0) for pg in pages) * (1.0 / MOBA_BLOCK)
        kb = jnp.concatenate([flat(k_refs[pg]) for pg in pages], axis=0)
        vb = jnp.concatenate([flat(v_refs[pg]) for pg in pages], axis=0)
        s = _dot_nt(q_sc[...], kb) * scale
        qhead = lax.broadcasted_iota(jnp.int32, s.shape, 0) % n_heads
        khead = lax.broadcasted_iota(jnp.int32, s.shape, 1) % n_heads
        s = jnp.where(qhead == khead, s, NEG)
        m = jnp.max(s, axis=-1, keepdims=True)
        p = jnp.exp(s - m)
        l = jnp.sum(p, axis=-1, keepdims=True)
        op_ref[u] = jnp.dot(p.astype(BF16), vb, preferred_element_type=F32)
        blk = j * group + u
        ms_ref[...] = jnp.where(lane == blk, m, ms_ref[...])
        ls_ref[...] = jnp.where(lane == blk, l, ls_ref[...])


def moba_pages(page_table, q, cache_k, cache_v, layer, gq, *, n_heads):
    batch, n_pages = page_table.shape
    nblk = n_pages // PAGES_PER_BLOCK
    group = math.gcd(nblk, BLOCKS_PER_STEP)
    n_pg = group * PAGES_PER_BLOCK
    d = HEAD_DIM
    rows = q.shape[1]
    stat_lanes = 128 * pl.cdiv(nblk, 128)
    page = lambda o: pl.BlockSpec((None, None, PAGE_SIZE, n_heads, d),
                                  lambda b, j, pt, o=o: (layer, pt[b * n_pages + n_pg * j + o], 0, 0, 0))
    grid_spec = pltpu.PrefetchScalarGridSpec(
        num_scalar_prefetch=1,
        grid=(batch, nblk // group),
        in_specs=[pl.BlockSpec((None, rows, d), lambda b, j, pt: (b, 0, 0))]
                 + [page(o) for o in range(n_pg)] + [page(o) for o in range(n_pg)]
                 + [pl.BlockSpec((1, d), lambda b, j, pt: (0, 0))],
        out_specs=[pl.BlockSpec((None, group, n_heads, d), lambda b, j, pt: (b, j, 0, 0)),
                   pl.BlockSpec((None, rows, stat_lanes), lambda b, j, pt: (b, 0, 0)),
                   pl.BlockSpec((None, rows, stat_lanes), lambda b, j, pt: (b, 0, 0)),
                   pl.BlockSpec((None, group, rows, d), lambda b, j, pt: (b, j, 0, 0))],
        scratch_shapes=[pltpu.VMEM((rows, d), BF16)],
    )
    return pl.pallas_call(
        functools.partial(_moba_pages_kernel, n_heads=n_heads, group=group),
        grid_spec=grid_spec,
        out_shape=[jax.ShapeDtypeStruct((batch, nblk, n_heads, d), F32),
                   jax.ShapeDtypeStruct((batch, rows, stat_lanes), F32),
                   jax.ShapeDtypeStruct((batch, rows, stat_lanes), F32),
                   jax.ShapeDtypeStruct((batch, nblk, rows, d), F32)],
        compiler_params=_cparams(("parallel", "arbitrary")),
        name="moba_pages",
    )(page_table.reshape(-1), q, *([cache_k] * n_pg), *([cache_v] * n_pg), gq.reshape(1, d))


def _moba_combine_kernel(q_ref, kn_ref, vn_ref, km_ref, ms_ref, ls_ref, op_ref, gq_ref, gk_ref,
                         ha_ref, ko_ref, *, n_heads, nblk):
    scale = HEAD_DIM ** -0.5
    q = _rms_rows(q_ref[...], gq_ref[...])
    kn = _rms_rows(kn_ref[...], gk_ref[...])
    ko_ref[...] = kn

    km = km_ref[...].reshape(nblk * n_heads, HEAD_DIM)
    g_all = _dot_hp(q, km, _dot_nt)
    rhead = lax.broadcasted_iota(jnp.int32, g_all.shape, 0) % n_heads
    chead = lax.broadcasted_iota(jnp.int32, g_all.shape, 1) % n_heads
    g_all = jnp.where(rhead == chead, g_all, 0.0)
    pick = jnp.where(lax.broadcasted_iota(jnp.int32, (nblk * n_heads, nblk), 0) // n_heads
                     == lax.broadcasted_iota(jnp.int32, (nblk * n_heads, nblk), 1), 1.0, 0.0).astype(BF16)
    gate = _dot_hp_exact_rhs(g_all, pick, _dot_nn)

    lane = lax.broadcasted_iota(jnp.int32, gate.shape, 1)
    rank = jnp.zeros(gate.shape, F32)
    for j in range(nblk):
        gj = gate[:, j:j + 1]
        beats = (gj > gate) | ((gj == gate) & (lane > j))
        rank = rank + jnp.where(beats, 1.0, 0.0)
    sel = rank < min(MOBA_TOPK, nblk)

    s_own = _dot_nt(q.astype(BF16), kn.astype(BF16)) * scale
    r = lax.broadcasted_iota(jnp.int32, s_own.shape, 0)
    c = lax.broadcasted_iota(jnp.int32, s_own.shape, 1)
    s_own = jnp.where((r % n_heads == c % n_heads) & (c // n_heads <= r // n_heads), s_own, NEG)

    ms = ms_ref[:, :nblk]
    ls = ls_ref[:, :nblk]
    m_star = jnp.maximum(jnp.max(jnp.where(sel, ms, NEG), axis=-1, keepdims=True),
                         jnp.max(s_own, axis=-1, keepdims=True))
    w = jnp.where(sel, jnp.exp(ms - m_star), 0.0)
    p_own = jnp.exp(s_own - m_star)
    l = jnp.sum(w * ls, axis=-1, keepdims=True) + jnp.sum(p_own, axis=-1, keepdims=True)
    o = jnp.dot(p_own.astype(BF16), vn_ref[...].astype(BF16), preferred_element_type=F32)
    for j in range(nblk):
        o = o + w[:, j:j + 1] * op_ref[j]
    ha_ref[...] = (o / l).astype(ha_ref.dtype)


def moba_combine(q, kn, vn, k_mean, ms, ls, op, gq, gk, *, n_heads):
    batch, nblk = k_mean.shape[:2]
    d = HEAD_DIM
    rows = q.shape[1]
    full = lambda a: pl.BlockSpec((None,) + a.shape[1:], lambda b: (b,) + (0,) * (a.ndim - 1))
    vec = pl.BlockSpec((1, d), lambda b: (0, 0))
    return pl.pallas_call(
        functools.partial(_moba_combine_kernel, n_heads=n_heads, nblk=nblk),
        grid=(batch,),
        in_specs=[full(q), full(kn), full(vn), full(k_mean), full(ms), full(ls), full(op), vec, vec],
        out_specs=[pl.BlockSpec((None, rows, d), lambda b: (b, 0, 0)),
                   pl.BlockSpec((None, rows, d), lambda b: (b, 0, 0))],
        out_shape=[jax.ShapeDtypeStruct((batch, rows, d), BF16),
                   jax.ShapeDtypeStruct((batch, rows, d), F32)],
        compiler_params=_cparams(("parallel",)),
        name="moba_combine",
    )(q, kn, vn, k_mean, ms, ls, op, gq.reshape(1, d), gk.reshape(1, d))


CONV_PAD = 8


def _pad_rows(x, rows):
    return jnp.pad(x, ((0, 0), (0, rows - x.shape[1]), (0, 0)))


def kernel(x_prompt, x_sample, cache_k, cache_v, state_mlstm_c, state_mlstm_n, state_mlstm_m, state_conv,
           page_table, p_prompt, p_sample, g_ffn1, w1_a, w3_a, w2_a, g_mix, w_in, b_if, q_norm_g, k_norm_g,
           h_norm_g, conv_w, w_out, g_ffn2, w1_b, w3_b, w2_b, g_ple, w_pgate, w_ple):
    bp, seq, d_model = x_prompt.shape
    bs, t_new, _ = x_sample.shape
    depth = g_ffn1.shape[0]
    h_m = state_mlstm_c.shape[2]
    h_a = cache_k.shape[3]
    d_m, d_a = h_m * HEAD_DIM, h_a * HEAD_DIM
    d_c = d_model - d_m - d_a
    n_p, n_s = bp * seq, bs * t_new
    assert 2 <= t_new <= CONV_PAD and (t_new * h_a) % 8 == 0
    assert seq % MLSTM_CHUNK == 0 and seq % MOBA_BLOCK == 0

    g0 = 4 * d_m
    g1 = g0 + 2 * h_m
    n_gate_lanes = 128
    col_aq = g0 // HEAD_DIM
    col_cb = (g0 + 3 * d_a) // d_c

    h = jnp.concatenate([x_prompt.reshape(n_p, d_model), x_sample.reshape(n_s, d_model)], axis=0)
    outs = {k: [] for k in ("kp", "vp", "ks", "vs", "cp", "np", "mp", "cs", "ns", "ms", "cvp", "cvs")}
    s_chunk = 128

    ffn_split = 2 if (w2_a.shape[1] // 2) % 128 == 0 else 1
    w_in_t = jnp.swapaxes(w_in, 1, 2)
    for l in range(depth):
        g = gate_up(rmsnorm_bf16(h, g_ffn1[l]), w1_a, w3_a, l)
        h = mm_resid(g, w2_a, l, h, alpha=0.5, max_tile=ROW_TILE_DOWN, k_split=ffn_split)

        b_gate = jnp.pad(b_if[l], (0, n_gate_lanes - 2 * h_m)).reshape(1, n_gate_lanes)
        u, gates = rmsnorm_gates(h, g_mix[l], w_in_t, l, g0, b_gate)
        z = in_proj(u, w_in_t, l, n_head=g0, shift=2 * h_m)

        gr_p = gates[:n_p, :2 * h_m].reshape(bp, seq, 2 * h_m).transpose(0, 2, 1)
        hm_p, c_p, nn_p, m_p = mlstm(
            z, gates, gr_p, jnp.zeros((bp, h_m, HEAD_DIM, HEAD_DIM), F32), jnp.zeros((bp, h_m, HEAD_DIM), F32),
            jnp.zeros((bp, h_m), F32), h_norm_g[l], batch=bp, seq=seq, chunk=MLSTM_CHUNK)
        kv_prev = (outs["kp"][0], outs["vp"][0]) if (depth == 2 and l == 1) else None
        ha_p, k_p, v_p = moba_prompt(z, col_aq, q_norm_g[l], k_norm_g[l], batch=bp, seq=seq, n_heads=h_a,
                                     prev=kv_prev)
        hc_p, cv_p = short_conv(z, col_cb, jnp.zeros((bp, 2, d_c), F32), conv_w[l],
                                batch=bp, seq=seq, rows=512, valid=seq)

        zs = z[n_p:].reshape(bs, t_new, -1)
        gs = gates[n_p:].reshape(bs, t_new, n_gate_lanes)
        pad_gate = jnp.concatenate([jnp.full((h_m,), NEG, F32), jnp.full((h_m,), -NEG, F32),
                                    jnp.zeros((n_gate_lanes - 2 * h_m,), F32)])
        gs_pad = jnp.concatenate([gs, jnp.broadcast_to(pad_gate, (bs, s_chunk - t_new, n_gate_lanes))], axis=1)
        hm_s, c_s, nn_s, m_s = mlstm(
            _pad_rows(zs[:, :, :g0], s_chunk).reshape(bs * s_chunk, g0), gs_pad.reshape(bs * s_chunk, n_gate_lanes),
            gs_pad[:, :, :2 * h_m].transpose(0, 2, 1), state_mlstm_c[l], state_mlstm_n[l], state_mlstm_m[l],
            h_norm_g[l], batch=bs, seq=s_chunk, chunk=s_chunk)
        hm_s = hm_s.reshape(bs, s_chunk, d_m)[:, :t_new].reshape(n_s, d_m)

        heads = lambda a: a.reshape(bs, t_new * h_a, HEAD_DIM)
        q_s = heads(zs[:, :, g0:g0 + d_a])
        kn_s = heads(zs[:, :, g0 + d_a:g0 + 2 * d_a])
        vn_s = heads(zs[:, :, g0 + 2 * d_a:g0 + 3 * d_a])
        k_mean, ms, ls, op = moba_pages(page_table, q_s, cache_k, cache_v, l, q_norm_g[l], n_heads=h_a)
        ha_s, k_s = moba_combine(q_s, kn_s, vn_s, k_mean, ms, ls, op, q_norm_g[l], k_norm_g[l], n_heads=h_a)
        ha_s = ha_s.reshape(n_s, d_a)
        k_s = k_s.reshape(bs, t_new, h_a, HEAD_DIM)
        v_s = vn_s.reshape(bs, t_new, h_a, HEAD_DIM)

        zc = _pad_rows(zs[:, :, g0 + 3 * d_a:], CONV_PAD).reshape(bs * CONV_PAD, 3 * d_c)
        hc_s, cv_s = short_conv(zc, 0, state_conv[l], conv_w[l], batch=bs, seq=CONV_PAD, rows=CONV_PAD,
                                valid=t_new)
        hc_s = hc_s.reshape(bs, CONV_PAD, d_c)[:, :t_new].reshape(n_s, d_c)

        mix = jnp.concatenate([jnp.concatenate([hm_p, ha_p, hc_p], axis=1),
                               jnp.concatenate([hm_s, ha_s, hc_s], axis=1)], axis=0)
        h = mm_resid(mix, w_out, l, h, alpha=1.0, max_tile=ROW_TILE)

        g = gate_up(rmsnorm_bf16(h, g_ffn2[l]), w1_b, w3_b, l)
        h = mm_resid(g, w2_b, l, h, alpha=0.5, max_tile=ROW_TILE_DOWN, k_split=ffn_split)

        n4 = rmsnorm_bf16(h, g_ple[l])
        pp, ps = p_prompt[l].reshape(n_p, -1), p_sample[l].reshape(n_s, -1)
        if l < depth - 1:
            h = ple(n4, w_pgate, jnp.concatenate([pp, ps], axis=0), w_ple, l, h)
        else:
            y_p = ple(n4, w_pgate, pp, w_ple, l, h, row0=0, rows=n_p)
            y_s = ple(n4, w_pgate, ps, w_ple, l, h, row0=n_p, rows=n_s)

        outs["kp"].append(k_p)
        outs["vp"].append(v_p)
        outs["ks"].append(k_s)
        outs["vs"].append(v_s)
        outs["cp"].append(c_p); outs["np"].append(nn_p); outs["mp"].append(m_p)
        outs["cs"].append(c_s); outs["ns"].append(nn_s); outs["ms"].append(m_s)
        outs["cvp"].append(cv_p); outs["cvs"].append(cv_s)

    st = lambda k: jnp.stack(outs[k])
    kv_p = [(outs[k][1] if depth == 2 else st(k)).reshape(depth, bp, seq, h_a, HEAD_DIM) for k in ("kp", "vp")]
    return (y_p.reshape(bp, seq, d_model), y_s.reshape(bs, t_new, d_model),
            kv_p[0], kv_p[1], st("ks"), st("vs"),
            st("cp"), st("np"), st("mp"), st("cs"), st("ns"), st("ms"), st("cvp"), st("cvs"))
```

```python
import functools
import math

import jax
import jax.numpy as jnp
from jax import lax
from jax.experimental import pallas as pl
from jax.experimental.pallas import tpu as pltpu

F32 = jnp.float32
BF16 = jnp.bfloat16

HEAD_DIM = 128
PAGE_SIZE = 128
MOBA_BLOCK = 256
MOBA_TOPK = 3
RMS_EPS = 1e-6
NEG = -1e30
MLSTM_CHUNK = 256
V7X_VMEM_LIMIT = 60000 * 1024


def _cparams(sem, vmem=V7X_VMEM_LIMIT):
    return pltpu.CompilerParams(dimension_semantics=sem, vmem_limit_bytes=vmem)


def _split3(x):
    hi = x.astype(BF16)
    r1 = x - hi.astype(F32)
    mid = r1.astype(BF16)
    lo = (r1 - mid.astype(F32)).astype(BF16)
    return hi, mid, lo


def _dot_nn(a, b):
    return jnp.dot(a, b, preferred_element_type=F32)


def _dot_nt(a, b):
    return lax.dot_general(a, b, (((1,), (1,)), ((), ())), preferred_element_type=F32)


def _dot_tn(a, b):
    return lax.dot_general(a, b, (((0,), (0,)), ((), ())), preferred_element_type=F32)


def _dot_hp(a, b, dot):
    a3, b3 = _split3(a), _split3(b)
    out = None
    for i, j in ((2, 0), (0, 2), (1, 1), (1, 0), (0, 1), (0, 0)):
        t = dot(a3[i], b3[j])
        out = t if out is None else out + t
    return out


def _dot_3x(a, b, dot):
    a_hi, a_mid, _ = _split3(a)
    b_hi, b_mid, _ = _split3(b)
    return dot(a_mid, b_hi) + dot(a_hi, b_mid) + dot(a_hi, b_hi)


def _dot_hp_exact_lhs(a_bf16, b, dot):
    b3 = _split3(b)
    return dot(a_bf16, b3[2]) + dot(a_bf16, b3[1]) + dot(a_bf16, b3[0])


def _dot_hp_exact_rhs(a, b_bf16, dot):
    a3 = _split3(a)
    return dot(a3[2], b_bf16) + dot(a3[1], b_bf16) + dot(a3[0], b_bf16)


def _rms_rows(x, g):
    return x * lax.rsqrt(jnp.mean(x * x, axis=-1, keepdims=True) + RMS_EPS) * g


def _log_sigmoid(x):
    return jnp.minimum(x, 0.0) - jnp.log1p(jnp.exp(-jnp.abs(x)))


def _sigmoid(x):
    return 1.0 / (1.0 + jnp.exp(-x))


def _rmsnorm_kernel(x_ref, g_ref, o_ref):
    o_ref[...] = _rms_rows(x_ref[...], g_ref[...]).astype(o_ref.dtype)


def _rmsnorm_gates_kernel(x_ref, g_ref, wg_ref, bg_ref, o_ref, gates_ref):
    y = _rms_rows(x_ref[...], g_ref[...])
    o_ref[...] = y.astype(o_ref.dtype)
    gates_ref[...] = _dot_3x(y, wg_ref[...], _dot_nt) + bg_ref[...]


def rmsnorm_bf16(x, g, *, tm=512):
    m, d = x.shape
    return pl.pallas_call(
        _rmsnorm_kernel,
        grid=(pl.cdiv(m, tm),),
        in_specs=[pl.BlockSpec((tm, d), lambda i: (i, 0)),
                  pl.BlockSpec((1, d), lambda i: (0, 0))],
        out_specs=pl.BlockSpec((tm, d), lambda i: (i, 0)),
        out_shape=jax.ShapeDtypeStruct((m, d), BF16),
        compiler_params=_cparams(("parallel",)),
        name="rmsnorm",
    )(x, g.reshape(1, d))


def rmsnorm_gates(x, g, wt, layer, col0, bg, *, tm=512):
    m, d = x.shape
    ng = bg.shape[1]
    assert col0 % ng == 0
    return pl.pallas_call(
        _rmsnorm_gates_kernel,
        grid=(pl.cdiv(m, tm),),
        in_specs=[pl.BlockSpec((tm, d), lambda i: (i, 0)),
                  pl.BlockSpec((1, d), lambda i: (0, 0)),
                  pl.BlockSpec((None, ng, d), lambda i: (layer, col0 // ng, 0)),
                  pl.BlockSpec((1, ng), lambda i: (0, 0))],
        out_specs=[pl.BlockSpec((tm, d), lambda i: (i, 0)),
                   pl.BlockSpec((tm, ng), lambda i: (i, 0))],
        out_shape=[jax.ShapeDtypeStruct((m, d), BF16),
                   jax.ShapeDtypeStruct((m, ng), F32)],
        compiler_params=_cparams(("parallel",)),
        name="rmsnorm_gates",
    )(x, g.reshape(1, d), wt, bg)


BF16_SUBLANES = 16
ROW_TILE = 1040
ROW_TILE_GATE_UP = 1648
ROW_TILE_IN = 1376
ROW_TILE_DOWN = 832


def _row_tile(m, max_tile):
    n = pl.cdiv(m, max_tile)
    return BF16_SUBLANES * pl.cdiv(pl.cdiv(m, n), BF16_SUBLANES)


def _row_cases(m, tm, body):
    nm = pl.cdiv(m, tm)
    rem = m - (nm - 1) * tm
    if rem == tm:
        body(tm)
        return
    i = pl.program_id(1)

    @pl.when(i < nm - 1)
    def _():
        body(tm)

    @pl.when(i == nm - 1)
    def _():
        body(rem)


def _cast_weights_once(pairs):
    @pl.when(pl.program_id(1) == 0)
    def _():
        for w_ref, w_sc in pairs:
            w_sc[...] = w_ref[...].astype(BF16)


def _gate_up_kernel(a_ref, w1_ref, w3_ref, o_ref, w1_sc, w3_sc, *, m, tm):
    _cast_weights_once(((w1_ref, w1_sc), (w3_ref, w3_sc)))

    def body(rows):
        a = a_ref[:rows, :]
        x1 = jnp.dot(a, w1_sc[...], preferred_element_type=F32)
        x3 = jnp.dot(a, w3_sc[...], preferred_element_type=F32)
        o_ref[:rows, :] = (x1 * _sigmoid(x1) * x3).astype(o_ref.dtype)
    _row_cases(m, tm, body)


def gate_up(a, w1, w3, layer, *, tn=256):
    m, k = a.shape
    n = w1.shape[2]
    tm = _row_tile(m, ROW_TILE_GATE_UP)
    wspec = pl.BlockSpec((None, k, tn), lambda j, i: (layer, 0, j))
    return pl.pallas_call(
        functools.partial(_gate_up_kernel, m=m, tm=tm),
        grid=(pl.cdiv(n, tn), pl.cdiv(m, tm)),
        in_specs=[pl.BlockSpec((tm, k), lambda j, i: (i, 0)), wspec, wspec],
        out_specs=pl.BlockSpec((tm, tn), lambda j, i: (i, j)),
        out_shape=jax.ShapeDtypeStruct((m, n), BF16),
        scratch_shapes=[pltpu.VMEM((k, tn), BF16), pltpu.VMEM((k, tn), BF16)],
        compiler_params=_cparams(("parallel", "arbitrary")),
        name="gate_up",
    )(a, w1, w3)


def _mm_resid_kernel(a_ref, w_ref, h_ref, o_ref, w_sc, *, m, tm, alpha):
    _cast_weights_once(((w_ref, w_sc),))

    def body(rows):
        acc = jnp.dot(a_ref[:rows, :], w_sc[...], preferred_element_type=F32)
        o_ref[:rows, :] = h_ref[:rows, :] + alpha * acc
    _row_cases(m, tm, body)


def mm_resid(a, w, layer, h, *, alpha, max_tile, tn=512, k_split=1):
    m, k = a.shape
    n = w.shape[2]
    tm = _row_tile(m, max_tile)
    kc = k // k_split
    assert kc * k_split == k and kc % 128 == 0
    for c in range(k_split):
        h = pl.pallas_call(
            functools.partial(_mm_resid_kernel, m=m, tm=tm, alpha=alpha),
            grid=(pl.cdiv(n, tn), pl.cdiv(m, tm)),
            in_specs=[pl.BlockSpec((tm, kc), lambda j, i, c=c: (i, c)),
                      pl.BlockSpec((None, kc, tn), lambda j, i, c=c: (layer, c, j)),
                      pl.BlockSpec((tm, tn), lambda j, i: (i, j))],
            out_specs=pl.BlockSpec((tm, tn), lambda j, i: (i, j)),
            out_shape=jax.ShapeDtypeStruct((m, n), F32),
            scratch_shapes=[pltpu.VMEM((kc, tn), BF16)],
            compiler_params=_cparams(("parallel", "arbitrary")),
            name="mm_resid",
        )(a, w, h)
    return h


def _out_proj_kernel(*refs, n_pieces, n_prompt_tiles, n_s, alpha):
    prompt, sample = refs[:n_pieces], refs[n_pieces:2 * n_pieces]
    w_ref, h_ref, o_ref, w_sc = refs[2 * n_pieces:]
    _cast_weights_once(((w_ref, w_sc),))
    i = pl.program_id(1)

    def body(pieces, rows):
        acc, k0 = None, 0
        for p_ref in pieces:
            kw = p_ref.shape[1]
            t = jnp.dot(p_ref[:rows, :], w_sc[k0:k0 + kw, :], preferred_element_type=F32)
            acc = t if acc is None else acc + t
            k0 += kw
        o_ref[:rows, :] = h_ref[:rows, :] + alpha * acc

    @pl.when(i < n_prompt_tiles)
    def _():
        body(prompt, o_ref.shape[0])

    @pl.when(i == n_prompt_tiles)
    def _():
        body(sample, n_s)


def out_proj(prompt, sample, w, layer, h, *, alpha, tn=512):
    n_p, n_s = prompt[0].shape[0], sample[0].shape[0]
    k, n = w.shape[1], w.shape[2]
    tm = _row_tile(n_p, ROW_TILE)
    assert n_p % tm == 0 and n_s <= tm and sum(p.shape[1] for p in prompt) == k
    npt = n_p // tm
    p_specs = [pl.BlockSpec((tm, p.shape[1]), lambda j, i: (jnp.minimum(i, npt - 1), 0)) for p in prompt]
    s_specs = [pl.BlockSpec((n_s, s.shape[1]), lambda j, i: (0, 0)) for s in sample]
    return pl.pallas_call(
        functools.partial(_out_proj_kernel, n_pieces=len(prompt), n_prompt_tiles=npt, n_s=n_s, alpha=alpha),
        grid=(pl.cdiv(n, tn), npt + 1),
        in_specs=p_specs + s_specs + [pl.BlockSpec((None, k, tn), lambda j, i: (layer, 0, j)),
                                      pl.BlockSpec((tm, tn), lambda j, i: (i, j))],
        out_specs=pl.BlockSpec((tm, tn), lambda j, i: (i, j)),
        out_shape=jax.ShapeDtypeStruct((n_p + n_s, n), F32),
        scratch_shapes=[pltpu.VMEM((k, tn), BF16)],
        compiler_params=_cparams(("parallel", "arbitrary")),
        name="out_proj",
    )(*prompt, *sample, w, h)


def _in_proj_kernel(a_ref, w_ref, wx_ref, o_ref, w_sc, *, m, tm, n_head_tiles, shift):
    j = pl.program_id(0)
    tn = w_sc.shape[0]

    @pl.when((pl.program_id(1) == 0) & (j < n_head_tiles))
    def _():
        w_sc[...] = w_ref[...].astype(BF16)

    @pl.when((pl.program_id(1) == 0) & (j >= n_head_tiles))
    def _():
        w_sc[:tn - shift, :] = w_ref[shift:, :].astype(BF16)
        w_sc[tn - shift:, :] = wx_ref[...].astype(BF16)

    def body(rows):
        o_ref[:rows, :] = _dot_nt(a_ref[:rows, :], w_sc[...])
    _row_cases(m, tm, body)


def in_proj(a, wt, layer, *, n_head, shift, tn=512):
    m, k = a.shape
    n = wt.shape[1] - shift
    tm = _row_tile(m, ROW_TILE_IN)
    assert n % tn == 0 and n_head % tn == 0 and shift % 8 == 0 and tn % shift == 0
    return pl.pallas_call(
        functools.partial(_in_proj_kernel, m=m, tm=tm, n_head_tiles=n_head // tn, shift=shift),
        grid=(n // tn, pl.cdiv(m, tm)),
        in_specs=[pl.BlockSpec((tm, k), lambda j, i: (i, 0)),
                  pl.BlockSpec((None, tn, k), lambda j, i: (layer, j, 0)),
                  pl.BlockSpec((None, shift, k), lambda j, i: (layer, (j + 1) * (tn // shift), 0))],
        out_specs=pl.BlockSpec((tm, tn), lambda j, i: (i, j)),
        out_shape=jax.ShapeDtypeStruct((m, n), F32),
        scratch_shapes=[pltpu.VMEM((tn, k), BF16)],
        compiler_params=_cparams(("parallel", "arbitrary")),
        name="in_proj",
    )(a, wt, wt)


def _ple_kernel(a_ref, wg_ref, p_ref, wp_ref, h_ref, o_ref, wg_sc, wp_sc, *, m, tm):
    _cast_weights_once(((wg_ref, wg_sc), (wp_ref, wp_sc)))

    def body(rows):
        gate = _sigmoid(jnp.dot(a_ref[:rows, :], wg_sc[...], preferred_element_type=F32))
        pe = jnp.dot(p_ref[:rows, :].astype(BF16), wp_sc[...], preferred_element_type=F32)
        o_ref[:rows, :] = h_ref[:rows, :] + gate * pe
    _row_cases(m, tm, body)


def ple(a, wg, p, wp, layer, h, *, row0=0, rows=None, tn=512):
    k = a.shape[1]
    m = a.shape[0] if rows is None else rows
    tm = _row_tile(m, ROW_TILE)
    n = wg.shape[2]
    kp = p.shape[1]
    assert row0 % tm == 0
    r0 = row0 // tm
    return pl.pallas_call(
        functools.partial(_ple_kernel, m=m, tm=tm),
        grid=(pl.cdiv(n, tn), pl.cdiv(m, tm)),
        in_specs=[pl.BlockSpec((tm, k), lambda j, i: (r0 + i, 0)),
                  pl.BlockSpec((None, k, tn), lambda j, i: (layer, 0, j)),
                  pl.BlockSpec((tm, kp), lambda j, i: (i, 0)),
                  pl.BlockSpec((None, kp, tn), lambda j, i: (layer, 0, j)),
                  pl.BlockSpec((tm, tn), lambda j, i: (r0 + i, j))],
        out_specs=pl.BlockSpec((tm, tn), lambda j, i: (i, j)),
        out_shape=jax.ShapeDtypeStruct((m, n), F32),
        scratch_shapes=[pltpu.VMEM((k, tn), BF16), pltpu.VMEM((kp, tn), BF16)],
        compiler_params=_cparams(("parallel", "arbitrary")),
        name="ple",
    )(a, wg, p, wp, h)


def _mlstm_kernel(zq_ref, zk_ref, zv_ref, zo_ref, gc_ref, gr_ref, c0_ref, n0_ref, m0_ref, hg_ref,
                  hm_ref, c1_ref, n1_ref, m1_ref, ct_sc, n_sc, m_sc, *, n_heads, chunk):
    c = pl.program_id(1)
    last = pl.num_programs(1) - 1
    L = chunk
    scale = HEAD_DIM ** -0.5

    @pl.when(c == 0)
    def _():
        for h in range(n_heads):
            ct_sc[h] = c0_ref[h].T
        n_sc[...] = n0_ref[...]
        m_sc[...] = m0_ref[...]

    row = lax.broadcasted_iota(jnp.int32, (L, L), 0)
    col = lax.broadcasted_iota(jnp.int32, (L, L), 1)
    causal = col <= row
    tri_lo = jnp.where(causal, 1.0, 0.0).astype(BF16)
    tri_up = jnp.where(row <= col, 1.0, 0.0).astype(BF16)

    gc = gc_ref[...]
    gr = gr_ref[...]
    f_cols = _dot_hp_exact_lhs(tri_lo, _log_sigmoid(gc), _dot_nn)
    f_rows = _dot_hp_exact_rhs(_log_sigmoid(gr), tri_up, _dot_nn)

    for h in range(n_heads):
        lanes = slice(h * HEAD_DIM, (h + 1) * HEAD_DIM)
        q = zq_ref[:, lanes]
        k = zk_ref[:, lanes] * scale
        v = zv_ref[:, lanes]
        fc = f_cols[:, n_heads + h:n_heads + h + 1]
        fr = f_rows[n_heads + h:n_heads + h + 1, :]
        ic = gc[:, h:h + 1]
        ir = gr[h:h + 1, :]
        m_prev = m_sc[h:h + 1, 0:1]
        n_prev = n_sc[h:h + 1, :]
        ct_prev = ct_sc[h]

        dmat = jnp.where(causal, fc - fr + ir, NEG)
        inter = fc + m_prev
        m_t = jnp.maximum(inter, jnp.max(dmat, axis=-1, keepdims=True))
        w_intra = jnp.exp(dmat - m_t)
        w_inter = jnp.exp(inter - m_t)
        qb, kb, vb = q.astype(BF16), k.astype(BF16), v.astype(BF16)
        a = w_intra * _dot_nt(qb, kb)
        num = (w_inter * jnp.dot(qb, ct_prev.astype(BF16), preferred_element_type=F32)
               + jnp.dot(a.astype(BF16), vb, preferred_element_type=F32))
        den = w_inter * jnp.sum(q * n_prev, axis=-1, keepdims=True) + jnp.sum(a, axis=-1, keepdims=True)
        hval = num / jnp.maximum(jnp.abs(den), jnp.exp(-m_t))

        m_end = m_t[L - 1:L, :]
        w_s = jnp.exp(fc[L - 1:L, :] - fc + ic - m_end)
        decay = jnp.exp(inter[L - 1:L, :] - m_end)
        kw = w_s * k
        ct_new = decay * ct_prev + _dot_tn(kw.astype(BF16), vb)
        n_new = decay * n_prev + jnp.sum(kw, axis=0, keepdims=True)
        ct_sc[h] = ct_new
        n_sc[h:h + 1, :] = n_new
        m_sc[h:h + 1, :] = jnp.broadcast_to(m_end, (1, HEAD_DIM))

        hn = _rms_rows(hval, hg_ref[:, lanes])
        hm_ref[:, lanes] = (hn * _sigmoid(zo_ref[:, lanes])).astype(hm_ref.dtype)

    @pl.when(c == last)
    def _():
        for h in range(n_heads):
            c1_ref[h] = ct_sc[h].T
        n1_ref[...] = n_sc[...]
        m1_ref[...] = m_sc[...]


def mlstm(zz, gates_c, gates_r, c0, n0, m0, h_norm_g, *, batch, seq, chunk):
    n_heads = c0.shape[1]
    dm = n_heads * HEAD_DIM
    nc = seq // chunk
    m0b = jnp.broadcast_to(m0[:, :, None], (batch, n_heads, HEAD_DIM))
    zspec = lambda g: pl.BlockSpec((chunk, dm), lambda b, c, g=g: (b * nc + c, g))
    state = lambda *shape: pl.BlockSpec((None,) + shape, lambda b, c: (b,) + (0,) * len(shape))
    hm, c1, n1, m1 = pl.pallas_call(
        functools.partial(_mlstm_kernel, n_heads=n_heads, chunk=chunk),
        grid=(batch, nc),
        in_specs=[zspec(0), zspec(1), zspec(2), zspec(3),
                  pl.BlockSpec((chunk, gates_c.shape[1]), lambda b, c: (b * nc + c, 0)),
                  pl.BlockSpec((None, 2 * n_heads, chunk), lambda b, c: (b, 0, c)),
                  state(n_heads, HEAD_DIM, HEAD_DIM), state(n_heads, HEAD_DIM), state(n_heads, HEAD_DIM),
                  pl.BlockSpec((1, dm), lambda b, c: (0, 0))],
        out_specs=[pl.BlockSpec((chunk, dm), lambda b, c: (b * nc + c, 0)),
                   state(n_heads, HEAD_DIM, HEAD_DIM), state(n_heads, HEAD_DIM), state(n_heads, HEAD_DIM)],
        out_shape=[jax.ShapeDtypeStruct((batch * seq, dm), BF16),
                   jax.ShapeDtypeStruct((batch, n_heads, HEAD_DIM, HEAD_DIM), F32),
                   jax.ShapeDtypeStruct((batch, n_heads, HEAD_DIM), F32),
                   jax.ShapeDtypeStruct((batch, n_heads, HEAD_DIM), F32)],
        scratch_shapes=[pltpu.VMEM((n_heads, HEAD_DIM, HEAD_DIM), F32),
                        pltpu.VMEM((n_heads, HEAD_DIM), F32),
                        pltpu.VMEM((n_heads, HEAD_DIM), F32)],
        compiler_params=_cparams(("parallel", "arbitrary")),
        name="mlstm",
    )(zz, zz, zz, zz, gates_c, gates_r, c0, n0, m0b, h_norm_g.reshape(1, dm))
    return hm, c1, n1, m1[:, :, 0]


def _conv_kernel(cb_ref, cc_ref, cx_ref, s0_ref, w_ref, hc_ref, s1_ref, carry_sc, *, rows, last_row):
    t = pl.program_id(1)
    last = pl.num_programs(1) - 1

    @pl.when(t == 0)
    def _():
        carry_sc[...] = s0_ref[...]

    u = cc_ref[...] * cx_ref[...]
    prev2 = carry_sc[0:1, :]
    prev1 = carry_sc[1:2, :]
    r = lax.broadcasted_iota(jnp.int32, u.shape, 0)
    u1 = jnp.where(r == 0, prev1, pltpu.roll(u, 1, 0))
    u2 = jnp.where(r == 0, prev2, jnp.where(r == 1, prev1, pltpu.roll(u, 2, 0)))
    y = w_ref[0:1, :] * u2 + w_ref[1:2, :] * u1 + w_ref[2:3, :] * u
    hc_ref[...] = (cb_ref[...] * y).astype(hc_ref.dtype)
    carry_sc[...] = u[rows - 2:rows, :]

    @pl.when(t == last)
    def _():
        s1_ref[...] = u[last_row - 1:last_row + 1, :]


def short_conv(zz, col0, s0, w, *, batch, seq, rows, valid):
    c = s0.shape[-1]
    nt = seq // rows
    last_row = valid - 1 - (nt - 1) * rows
    zspec = lambda g: pl.BlockSpec((rows, c), lambda b, t, g=g: (b * nt + t, col0 + g))
    return pl.pallas_call(
        functools.partial(_conv_kernel, rows=rows, last_row=last_row),
        grid=(batch, nt),
        in_specs=[zspec(0), zspec(1), zspec(2),
                  pl.BlockSpec((None, 2, c), lambda b, t: (b, 0, 0)),
                  pl.BlockSpec((3, c), lambda b, t: (0, 0))],
        out_specs=[pl.BlockSpec((rows, c), lambda b, t: (b * nt + t, 0)),
                   pl.BlockSpec((None, 2, c), lambda b, t: (b, 0, 0))],
        out_shape=[jax.ShapeDtypeStruct((batch * seq, c), BF16),
                   jax.ShapeDtypeStruct((batch, 2, c), F32)],
        scratch_shapes=[pltpu.VMEM((2, c), F32)],
        compiler_params=_cparams(("parallel", "arbitrary")),
        name="short_conv",
    )(zz, zz, zz, s0, w)


def _top_mask_rows(gate, valid, n_keep):
    n = gate.shape[0]
    jidx = lax.broadcasted_iota(jnp.int32, gate.shape, 0)
    g = jnp.where(valid, gate, NEG)
    rank = jnp.zeros(gate.shape, F32)
    for j in range(n):
        gj = g[j:j + 1, :]
        beats = (gj > g) | ((gj == g) & (jidx > j))
        rank = rank + jnp.where(beats, 1.0, 0.0)
    return jnp.where(valid & (rank < n_keep), 1.0, 0.0)


def _moba_prompt_kernel(q_ref, k_ref, v_ref, gq_ref, gk_ref, *refs, seq, stacked):
    nb = seq // MOBA_BLOCK
    scale = HEAD_DIM ** -0.5
    qn = _rms_rows(q_ref[...], gq_ref[...])
    kn = _rms_rows(k_ref[...], gk_ref[...])
    v = v_ref[...]
    if stacked:
        k_prev_ref, v_prev_ref, ko_ref, vo_ref, ha_ref = refs
        ko_ref[0] = k_prev_ref[...]
        vo_ref[0] = v_prev_ref[...]
        ko_ref[1] = kn
        vo_ref[1] = v
    else:
        ko_ref, vo_ref, ha_ref = refs
        ko_ref[...] = kn
        vo_ref[...] = v
    qb, kb, vb = qn.astype(BF16), kn.astype(BF16), v.astype(BF16)

    avg = jnp.where(lax.broadcasted_iota(jnp.int32, (HEAD_DIM, seq), 1) // MOBA_BLOCK
                    == lax.broadcasted_iota(jnp.int32, (HEAD_DIM, seq), 0), 1.0 / MOBA_BLOCK, 0.0).astype(BF16)
    k_mean = _dot_hp_exact_lhs(avg, kn, _dot_nn)
    gate_t = _dot_3x(k_mean, qn, _dot_nt)
    own = lax.broadcasted_iota(jnp.int32, gate_t.shape, 1) // MOBA_BLOCK
    blk = lax.broadcasted_iota(jnp.int32, gate_t.shape, 0)
    sel = _top_mask_rows(gate_t[:8 * pl.cdiv(nb, 8)], (blk < own)[:8 * pl.cdiv(nb, 8)], MOBA_TOPK)
    sel = jnp.concatenate([sel, jnp.zeros((HEAD_DIM - sel.shape[0], seq), F32)], axis=0).T

    r = lax.broadcasted_iota(jnp.int32, (MOBA_BLOCK, MOBA_BLOCK), 0)
    c = lax.broadcasted_iota(jnp.int32, (MOBA_BLOCK, MOBA_BLOCK), 1)
    for i in range(nb):
        rows = slice(i * MOBA_BLOCK, (i + 1) * MOBA_BLOCK)
        kv = (i + 1) * MOBA_BLOCK
        s = _dot_nt(qb[rows], kb[:kv]) * scale
        parts = [jnp.where(sel[rows, j:j + 1] > 0.5, s[:, j * MOBA_BLOCK:(j + 1) * MOBA_BLOCK], NEG)
                 for j in range(i)]
        parts.append(jnp.where(c <= r, s[:, i * MOBA_BLOCK:], NEG))
        s = parts[0] if i == 0 else jnp.concatenate(parts, axis=1)
        p = jnp.exp(s - jnp.max(s, axis=-1, keepdims=True))
        o = jnp.dot(p.astype(BF16), vb[:kv], preferred_element_type=F32) / jnp.sum(p, axis=-1, keepdims=True)
        ha_ref[rows, :] = o.astype(ha_ref.dtype)


def moba_prompt(zz, colq, gq, gk, *, batch, seq, n_heads, prev=None):
    d = HEAD_DIM
    zspec = lambda g: pl.BlockSpec((seq, d), lambda b, h, g=g: (b, colq + g * n_heads + h))
    kvspec = pl.BlockSpec((None, seq, d), lambda b, h: (b, 0, h))
    vec = pl.BlockSpec((1, d), lambda b, h: (0, 0))
    kv_shape = jax.ShapeDtypeStruct((batch, seq, n_heads * d), F32)
    in_specs, args = [zspec(0), zspec(1), zspec(2), vec, vec], [zz, zz, zz, gq.reshape(1, d), gk.reshape(1, d)]
    ospec = kvspec
    if prev is not None:
        in_specs += [kvspec, kvspec]
        args += list(prev)
        ospec = pl.BlockSpec((2, None, seq, d), lambda b, h: (0, b, 0, h))
        kv_shape = jax.ShapeDtypeStruct((2, batch, seq, n_heads * d), F32)
    ko, vo, ha = pl.pallas_call(
        functools.partial(_moba_prompt_kernel, seq=seq, stacked=prev is not None),
        grid=(batch, n_heads),
        in_specs=in_specs,
        out_specs=[ospec, ospec, pl.BlockSpec((seq, d), lambda b, h: (b, h))],
        out_shape=[kv_shape, kv_shape, jax.ShapeDtypeStruct((batch * seq, n_heads * d), BF16)],
        compiler_params=_cparams(("parallel", "parallel")),
        name="moba_prompt",
    )(*args)
    return ha, ko, vo


PAGES_PER_BLOCK = MOBA_BLOCK // PAGE_SIZE
BLOCKS_PER_STEP = 4


def _moba_pages_kernel(pt_ref, q_ref, *refs, n_heads, group):
    del pt_ref
    n_pg = group * PAGES_PER_BLOCK
    k_refs, v_refs = refs[:n_pg], refs[n_pg:2 * n_pg]
    gq_ref, km_ref, ms_ref, ls_ref, op_ref, q_sc = refs[2 * n_pg:]
    j = pl.program_id(1)
    scale = HEAD_DIM ** -0.5
    rows = q_sc.shape[0]

    @pl.when(j == 0)
    def _():
        q_sc[...] = _rms_rows(q_ref[...], gq_ref[...]).astype(BF16)
        ms_ref[...] = jnp.zeros(ms_ref.shape, F32)
        ls_ref[...] = jnp.zeros(ls_ref.shape, F32)

    flat = lambda r: r[...].reshape(PAGE_SIZE * n_heads, HEAD_DIM).astype(BF16)
    lane = lax.broadcasted_iota(jnp.int32, (rows, ms_ref.shape[-1]), 1)
    for u in range(group):
        pages = range(u * PAGES_PER_BLOCK, (u + 1) * PAGES_PER_BLOCK)
        km_ref[u] = sum(jnp.sum(k_refs[pg][...], axis=0) for pg in pages) * (1.0 / MOBA_BLOCK)
        kb = jnp.concatenate([flat(k_refs[pg]) for pg in pages], axis=0)
        vb = jnp.concatenate([flat(v_refs[pg]) for pg in pages], axis=0)
        s = _dot_nt(q_sc[...], kb) * scale
        qhead = lax.broadcasted_iota(jnp.int32, s.shape, 0) % n_heads
        khead = lax.broadcasted_iota(jnp.int32, s.shape, 1) % n_heads
        s = jnp.where(qhead == khead, s, NEG)
        m = jnp.max(s, axis=-1, keepdims=True)
        p = jnp.exp(s - m)
        l = jnp.sum(p, axis=-1, keepdims=True)
        op_ref[u] = jnp.dot(p.astype(BF16), vb, preferred_element_type=F32)
        blk = j * group + u
        ms_ref[...] = jnp.where(lane == blk, m, ms_ref[...])
        ls_ref[...] = jnp.where(lane == blk, l, ls_ref[...])


def moba_pages(page_table, q, cache_k, cache_v, layer, gq, *, n_heads):
    batch, n_pages = page_table.shape
    nblk = n_pages // PAGES_PER_BLOCK
    group = math.gcd(nblk, BLOCKS_PER_STEP)
    n_pg = group * PAGES_PER_BLOCK
    d = HEAD_DIM
    rows = q.shape[1]
    stat_lanes = 128 * pl.cdiv(nblk, 128)
    page = lambda o: pl.BlockSpec((None, None, PAGE_SIZE, n_heads, d),
                                  lambda b, j, pt, o=o: (layer, pt[b * n_pages + n_pg * j + o], 0, 0, 0))
    grid_spec = pltpu.PrefetchScalarGridSpec(
        num_scalar_prefetch=1,
        grid=(batch, nblk // group),
        in_specs=[pl.BlockSpec((None, rows, d), lambda b, j, pt: (b, 0, 0))]
                 + [page(o) for o in range(n_pg)] + [page(o) for o in range(n_pg)]
                 + [pl.BlockSpec((1, d), lambda b, j, pt: (0, 0))],
        out_specs=[pl.BlockSpec((None, group, n_heads, d), lambda b, j, pt: (b, j, 0, 0)),
                   pl.BlockSpec((None, rows, stat_lanes), lambda b, j, pt: (b, 0, 0)),
                   pl.BlockSpec((None, rows, stat_lanes), lambda b, j, pt: (b, 0, 0)),
                   pl.BlockSpec((None, group, rows, d), lambda b, j, pt: (b, j, 0, 0))],
        scratch_shapes=[pltpu.VMEM((rows, d), BF16)],
    )
    return pl.pallas_call(
        functools.partial(_moba_pages_kernel, n_heads=n_heads, group=group),
        grid_spec=grid_spec,
        out_shape=[jax.ShapeDtypeStruct((batch, nblk, n_heads, d), F32),
                   jax.ShapeDtypeStruct((batch, rows, stat_lanes), F32),
                   jax.ShapeDtypeStruct((batch, rows, stat_lanes), F32),
                   jax.ShapeDtypeStruct((batch, nblk, rows, d), F32)],
        compiler_params=_cparams(("parallel", "arbitrary")),
        name="moba_pages",
    )(page_table.reshape(-1), q, *([cache_k] * n_pg), *([cache_v] * n_pg), gq.reshape(1, d))


def _moba_combine_kernel(q_ref, kn_ref, vn_ref, km_ref, ms_ref, ls_ref, op_ref, gq_ref, gk_ref,
                         ha_ref, ko_ref, *, n_heads, nblk):
    scale = HEAD_DIM ** -0.5
    q = _rms_rows(q_ref[...], gq_ref[...])
    kn = _rms_rows(kn_ref[...], gk_ref[...])
    ko_ref[...] = kn

    km = km_ref[...].reshape(nblk * n_heads, HEAD_DIM)
    g_all = _dot_hp(q, km, _dot_nt)
    rhead = lax.broadcasted_iota(jnp.int32, g_all.shape, 0) % n_heads
    chead = lax.broadcasted_iota(jnp.int32, g_all.shape, 1) % n_heads
    g_all = jnp.where(rhead == chead, g_all, 0.0)
    pick = jnp.where(lax.broadcasted_iota(jnp.int32, (nblk * n_heads, nblk), 0) // n_heads
                     == lax.broadcasted_iota(jnp.int32, (nblk * n_heads, nblk), 1), 1.0, 0.0).astype(BF16)
    gate = _dot_hp_exact_rhs(g_all, pick, _dot_nn)

    lane = lax.broadcasted_iota(jnp.int32, gate.shape, 1)
    rank = jnp.zeros(gate.shape, F32)
    for j in range(nblk):
        gj = gate[:, j:j + 1]
        beats = (gj > gate) | ((gj == gate) & (lane > j))
        rank = rank + jnp.where(beats, 1.0, 0.0)
    sel = rank < min(MOBA_TOPK, nblk)

    s_own = _dot_nt(q.astype(BF16), kn.astype(BF16)) * scale
    r = lax.broadcasted_iota(jnp.int32, s_own.shape, 0)
    c = lax.broadcasted_iota(jnp.int32, s_own.shape, 1)
    s_own = jnp.where((r % n_heads == c % n_heads) & (c // n_heads <= r // n_heads), s_own, NEG)

    ms = ms_ref[:, :nblk]
    ls = ls_ref[:, :nblk]
    m_star = jnp.maximum(jnp.max(jnp.where(sel, ms, NEG), axis=-1, keepdims=True),
                         jnp.max(s_own, axis=-1, keepdims=True))
    w = jnp.where(sel, jnp.exp(ms - m_star), 0.0)
    p_own = jnp.exp(s_own - m_star)
    l = jnp.sum(w * ls, axis=-1, keepdims=True) + jnp.sum(p_own, axis=-1, keepdims=True)
    o = jnp.dot(p_own.astype(BF16), vn_ref[...].astype(BF16), preferred_element_type=F32)
    for j in range(nblk):
        o = o + w[:, j:j + 1] * op_ref[j]
    ha_ref[...] = (o / l).astype(ha_ref.dtype)


def moba_combine(q, kn, vn, k_mean, ms, ls, op, gq, gk, *, n_heads):
    batch, nblk = k_mean.shape[:2]
    d = HEAD_DIM
    rows = q.shape[1]
    full = lambda a: pl.BlockSpec((None,) + a.shape[1:], lambda b: (b,) + (0,) * (a.ndim - 1))
    vec = pl.BlockSpec((1, d), lambda b: (0, 0))
    return pl.pallas_call(
        functools.partial(_moba_combine_kernel, n_heads=n_heads, nblk=nblk),
        grid=(batch,),
        in_specs=[full(q), full(kn), full(vn), full(k_mean), full(ms), full(ls), full(op), vec, vec],
        out_specs=[pl.BlockSpec((None, rows, d), lambda b: (b, 0, 0)),
                   pl.BlockSpec((None, rows, d), lambda b: (b, 0, 0))],
        out_shape=[jax.ShapeDtypeStruct((batch, rows, d), BF16),
                   jax.ShapeDtypeStruct((batch, rows, d), F32)],
        compiler_params=_cparams(("parallel",)),
        name="moba_combine",
    )(q, kn, vn, k_mean, ms, ls, op, gq.reshape(1, d), gk.reshape(1, d))


CONV_PAD = 8


def _pad_rows(x, rows):
    return jnp.pad(x, ((0, 0), (0, rows - x.shape[1]), (0, 0)))


def kernel(x_prompt, x_sample, cache_k, cache_v, state_mlstm_c, state_mlstm_n, state_mlstm_m, state_conv,
           page_table, p_prompt, p_sample, g_ffn1, w1_a, w3_a, w2_a, g_mix, w_in, b_if, q_norm_g, k_norm_g,
           h_norm_g, conv_w, w_out, g_ffn2, w1_b, w3_b, w2_b, g_ple, w_pgate, w_ple):
    bp, seq, d_model = x_prompt.shape
    bs, t_new, _ = x_sample.shape
    depth = g_ffn1.shape[0]
    h_m = state_mlstm_c.shape[2]
    h_a = cache_k.shape[3]
    d_m, d_a = h_m * HEAD_DIM, h_a * HEAD_DIM
    d_c = d_model - d_m - d_a
    n_p, n_s = bp * seq, bs * t_new
    assert 2 <= t_new <= CONV_PAD and (t_new * h_a) % 8 == 0
    assert seq % MLSTM_CHUNK == 0 and seq % MOBA_BLOCK == 0

    g0 = 4 * d_m
    g1 = g0 + 2 * h_m
    n_gate_lanes = 128
    col_aq = g0 // HEAD_DIM
    col_cb = (g0 + 3 * d_a) // d_c

    h = jnp.concatenate([x_prompt.reshape(n_p, d_model), x_sample.reshape(n_s, d_model)], axis=0)
    outs = {k: [] for k in ("kp", "vp", "ks", "vs", "cp", "np", "mp", "cs", "ns", "ms", "cvp", "cvs")}
    s_chunk = 128

    ffn_split = 2 if (w2_a.shape[1] // 2) % 128 == 0 else 1
    w_in_t = jnp.swapaxes(w_in, 1, 2)
    for l in range(depth):
        g = gate_up(rmsnorm_bf16(h, g_ffn1[l]), w1_a, w3_a, l)
        h = mm_resid(g, w2_a, l, h, alpha=0.5, max_tile=ROW_TILE_DOWN, k_split=ffn_split)

        b_gate = jnp.pad(b_if[l], (0, n_gate_lanes - 2 * h_m)).reshape(1, n_gate_lanes)
        u, gates = rmsnorm_gates(h, g_mix[l], w_in_t, l, g0, b_gate)
        z = in_proj(u, w_in_t, l, n_head=g0, shift=2 * h_m)

        gr_p = gates[:n_p, :2 * h_m].reshape(bp, seq, 2 * h_m).transpose(0, 2, 1)
        hm_p, c_p, nn_p, m_p = mlstm(
            z, gates, gr_p, jnp.zeros((bp, h_m, HEAD_DIM, HEAD_DIM), F32), jnp.zeros((bp, h_m, HEAD_DIM), F32),
            jnp.zeros((bp, h_m), F32), h_norm_g[l], batch=bp, seq=seq, chunk=MLSTM_CHUNK)
        kv_prev = (outs["kp"][0], outs["vp"][0]) if (depth == 2 and l == 1) else None
        ha_p, k_p, v_p = moba_prompt(z, col_aq, q_norm_g[l], k_norm_g[l], batch=bp, seq=seq, n_heads=h_a,
                                     prev=kv_prev)
        hc_p, cv_p = short_conv(z, col_cb, jnp.zeros((bp, 2, d_c), F32), conv_w[l],
                                batch=bp, seq=seq, rows=512, valid=seq)

        zs = z[n_p:].reshape(bs, t_new, -1)
        gs = gates[n_p:].reshape(bs, t_new, n_gate_lanes)
        pad_gate = jnp.concatenate([jnp.full((h_m,), NEG, F32), jnp.full((h_m,), -NEG, F32),
                                    jnp.zeros((n_gate_lanes - 2 * h_m,), F32)])
        gs_pad = jnp.concatenate([gs, jnp.broadcast_to(pad_gate, (bs, s_chunk - t_new, n_gate_lanes))], axis=1)
        hm_s, c_s, nn_s, m_s = mlstm(
            _pad_rows(zs[:, :, :g0], s_chunk).reshape(bs * s_chunk, g0), gs_pad.reshape(bs * s_chunk, n_gate_lanes),
            gs_pad[:, :, :2 * h_m].transpose(0, 2, 1), state_mlstm_c[l], state_mlstm_n[l], state_mlstm_m[l],
            h_norm_g[l], batch=bs, seq=s_chunk, chunk=s_chunk)
        hm_s = hm_s.reshape(bs, s_chunk, d_m)[:, :t_new].reshape(n_s, d_m)

        heads = lambda a: a.reshape(bs, t_new * h_a, HEAD_DIM)
        q_s = heads(zs[:, :, g0:g0 + d_a])
        kn_s = heads(zs[:, :, g0 + d_a:g0 + 2 * d_a])
        vn_s = heads(zs[:, :, g0 + 2 * d_a:g0 + 3 * d_a])
        k_mean, ms, ls, op = moba_pages(page_table, q_s, cache_k, cache_v, l, q_norm_g[l], n_heads=h_a)
        ha_s, k_s = moba_combine(q_s, kn_s, vn_s, k_mean, ms, ls, op, q_norm_g[l], k_norm_g[l], n_heads=h_a)
        ha_s = ha_s.reshape(n_s, d_a)
        k_s = k_s.reshape(bs, t_new, h_a, HEAD_DIM)
        v_s = vn_s.reshape(bs, t_new, h_a, HEAD_DIM)

        zc = _pad_rows(zs[:, :, g0 + 3 * d_a:], CONV_PAD).reshape(bs * CONV_PAD, 3 * d_c)
        hc_s, cv_s = short_conv(zc, 0, state_conv[l], conv_w[l], batch=bs, seq=CONV_PAD, rows=CONV_PAD,
                                valid=t_new)
        hc_s = hc_s.reshape(bs, CONV_PAD, d_c)[:, :t_new].reshape(n_s, d_c)

        h = out_proj((hm_p, ha_p, hc_p), (hm_s, ha_s, hc_s), w_out, l, h, alpha=1.0)

        g = gate_up(rmsnorm_bf16(h, g_ffn2[l]), w1_b, w3_b, l)
        h = mm_resid(g, w2_b, l, h, alpha=0.5, max_tile=ROW_TILE_DOWN, k_split=ffn_split)

        n4 = rmsnorm_bf16(h, g_ple[l])
        pp, ps = p_prompt[l].reshape(n_p, -1), p_sample[l].reshape(n_s, -1)
        if l < depth - 1:
            h = ple(n4, w_pgate, jnp.concatenate([pp, ps], axis=0), w_ple, l, h)
        else:
            y_p = ple(n4, w_pgate, pp, w_ple, l, h, row0=0, rows=n_p)
            y_s = ple(n4, w_pgate, ps, w_ple, l, h, row0=n_p, rows=n_s)

        outs["kp"].append(k_p)
        outs["vp"].append(v_p)
        outs["ks"].append(k_s)
        outs["vs"].append(v_s)
        outs["cp"].append(c_p); outs["np"].append(nn_p); outs["mp"].append(m_p)
        outs["cs"].append(c_s); outs["ns"].append(nn_s); outs["ms"].append(m_s)
        outs["cvp"].append(cv_p); outs["cvs"].append(cv_s)

    st = lambda k: jnp.stack(outs[k])
    kv_p = [(outs[k][1] if depth == 2 else st(k)).reshape(depth, bp, seq, h_a, HEAD_DIM) for k in ("kp", "vp")]
    return (y_p.reshape(bp, seq, d_model), y_s.reshape(bs, t_new, d_model),
            kv_p[0], kv_p[1], st("ks"), st("vs"),
            st("cp"), st("np"), st("mp"), st("cs"), st("ns"), st("ms"), st("cvp"), st("cvs"))
```
